```python
import jax, jax.numpy as jnp
from jax import lax
import numpy as np

D_MODEL = 2048
BATCH = 4
SEQ = 4096
DEPTH = 4

N_EVEN = (DEPTH + 1) // 2
N_ODD = DEPTH // 2
EPS = 1e-6
D_FF = 4 * D_MODEL
GROUP_DIM = 128
D_A = D_MODEL // 2
D_B = D_MODEL - D_A
N_A_GROUPS = D_A // GROUP_DIM
CHUNK = 128
CONV_WIDTH = 31
D_AB_IN = 2 * D_A + 2 * D_B
N_HEADS = 16
Q_RANK = 512
KV_RANK = 512
NOPE_DIM = 128
ROPE_DIM = 64
V_DIM = 128
QK_DIM = NOPE_DIM + ROPE_DIM
D_MLA_IN = Q_RANK + KV_RANK + ROPE_DIM
ROPE_THETA = 10000.0
Q_BLOCK = 128
ATTN_SCALE = QK_DIM ** -0.5

kernel_name = "hybrid_sgu_conv_mla_adaln_trunk"


def _rms(x, g):
    xf = x.astype(jnp.float32)
    y = xf * lax.rsqrt(jnp.mean(xf * xf, axis=-1, keepdims=True) + EPS)
    return (y * g.astype(jnp.float32)).astype(x.dtype)


def _layernorm(x, g, b):
    xf = x.astype(jnp.float32)
    mu = jnp.mean(xf, axis=-1, keepdims=True)
    var = jnp.mean(jnp.square(xf - mu), axis=-1, keepdims=True)
    y = (xf - mu) * lax.rsqrt(var + EPS)
    return (y * g.astype(jnp.float32) + b.astype(jnp.float32)).astype(x.dtype)


def _modulate(h, shift, scale):
    return h * (1 + scale[:, None, :]) + shift[:, None, :]


def _spatial_gating(u, v, v_norm_g, w_s, b_s):
    bsz, s, _ = u.shape
    shp = (bsz, s // CHUNK, CHUNK, N_A_GROUPS, GROUP_DIM)
    v = _rms(v.reshape(shp), v_norm_g)
    mask = jnp.tril(jnp.ones((CHUNK, CHUNK), dtype=w_s.dtype))
    mixed = jnp.einsum('gts,bnsgd->bntgd', w_s * mask, v) + b_s.T[None, None, :, :, None]
    return (u.reshape(shp) * mixed).reshape(bsz, s, D_A)


def _conformer_conv(a, g, conv_w, conv_b, ln_g, ln_b):
    y = a * jax.nn.sigmoid(g)
    y = lax.conv_general_dilated(y, conv_w[:, None, :], window_strides=(1,),
                                 padding=[(CONV_WIDTH - 1, 0)],
                                 dimension_numbers=('NWC', 'WIO', 'NWC'),
                                 feature_group_count=D_B) + conv_b
    return jax.nn.silu(_layernorm(y, ln_g, ln_b))


def _even_mixer(h, w_in, sgu_norm_g, sgu_w, sgu_b, conv_w, conv_b, ln_g, ln_b, w_out):
    proj = h @ w_in
    u, v, a, g = jnp.split(proj, [D_A, 2 * D_A, 2 * D_A + D_B], axis=-1)
    out_a = _spatial_gating(jax.nn.gelu(u), jax.nn.gelu(v), sgu_norm_g, sgu_w, sgu_b)
    out_b = _conformer_conv(a, g, conv_w, conv_b, ln_g, ln_b)
    return jnp.concatenate([out_a, out_b], axis=-1) @ w_out


def _rope_tables(s):
    pos = jnp.arange(s, dtype=jnp.float32)
    inv = ROPE_THETA ** (-jnp.arange(0, ROPE_DIM, 2, dtype=jnp.float32) / ROPE_DIM)
    ang = pos[:, None] * inv[None, :]
    return jnp.cos(ang), jnp.sin(ang)


def _apply_rope(x, cos, sin):
    xf = x.astype(jnp.float32)
    x1, x2 = jnp.split(xf, 2, axis=-1)
    return jnp.concatenate([x1 * cos - x2 * sin, x1 * sin + x2 * cos], axis=-1).astype(x.dtype)


def _segment_head_norm(t, g):
    return jnp.concatenate([_rms(t[..., :NOPE_DIM], g[:NOPE_DIM]),
                            _rms(t[..., NOPE_DIM:], g[NOPE_DIM:])], axis=-1)


def _block_causal_attention(q_nope, q_rope, k_nope, k_rope, v):
    s = q_nope.shape[1]
    outs = []
    for i in range(s // Q_BLOCK):
        q0, q1 = i * Q_BLOCK, (i + 1) * Q_BLOCK
        sc = (jnp.einsum('bqhd,bkhd->bhqk', q_nope[:, q0:q1], k_nope[:, :q1])
              + jnp.einsum('bqhr,bkr->bhqk', q_rope[:, q0:q1], k_rope[:, :q1]))
        sc = sc.astype(jnp.float32) * ATTN_SCALE
        qi = q0 + jnp.arange(Q_BLOCK)
        ki = jnp.arange(q1)
        sc = jnp.where(ki[None, :] <= qi[:, None], sc, -jnp.inf)
        p = jax.nn.softmax(sc, axis=-1).astype(v.dtype)
        outs.append(jnp.einsum('bhqk,bkhd->bqhd', p, v[:, :q1]))
    return jnp.concatenate(outs, axis=1)


def _mla_mixer(h, w_in, q_norm_g, kv_norm_g, w_uq, w_ukv, q_head_g, k_head_g, w_out):
    bsz, s, _ = h.shape
    proj = h @ w_in
    c_q, c_kv, k_rope = jnp.split(proj, [Q_RANK, Q_RANK + KV_RANK], axis=-1)
    q = (_rms(c_q, q_norm_g) @ w_uq).reshape(bsz, s, N_HEADS, QK_DIM)
    kv = (_rms(c_kv, kv_norm_g) @ w_ukv).reshape(bsz, s, N_HEADS, NOPE_DIM + V_DIM)
    k_nope, v = jnp.split(kv, [NOPE_DIM], axis=-1)
    q = _segment_head_norm(q, q_head_g)
    k_nope = _rms(k_nope, k_head_g[:NOPE_DIM])
    k_rope = _rms(k_rope, k_head_g[NOPE_DIM:])
    cos, sin = _rope_tables(s)
    q_rope = _apply_rope(q[..., NOPE_DIM:], cos[:, None, :], sin[:, None, :])
    k_rope = _apply_rope(k_rope, cos, sin)
    o = _block_causal_attention(q[..., :NOPE_DIM], q_rope, k_nope, k_rope, v)
    return o.reshape(bsz, s, N_HEADS * V_DIM) @ w_out


def setup_inputs(seed: int = 0) -> dict:
    key = jax.random.key(seed)
    ks = jax.random.split(key, 32)
    f = jnp.float32
    nrm = lambda k, shp, sc: jax.random.normal(k, shp, f) * sc
    d = D_MODEL
    return {
        "x": nrm(ks[0], (BATCH, SEQ, d), 1.0),
        "c": nrm(ks[1], (BATCH, d), 1.0),
        "norm1_g": 1.0 + nrm(ks[2], (DEPTH, d), 0.02),
        "norm2_g": 1.0 + nrm(ks[3], (DEPTH, d), 0.02),
        "ada_w": nrm(ks[4], (DEPTH, d, 6 * d), 0.5 * d ** -0.5),
        "ada_b": nrm(ks[5], (DEPTH, 6 * d), 0.01),
        "mlp_w1": nrm(ks[6], (DEPTH, d, D_FF), d ** -0.5),
        "mlp_w2": nrm(ks[7], (DEPTH, D_FF, d), D_FF ** -0.5),
        "ab_w_in": nrm(ks[8], (N_EVEN, d, D_AB_IN), d ** -0.5),
        "sgu_norm_g": 1.0 + nrm(ks[9], (N_EVEN, N_A_GROUPS, GROUP_DIM), 0.02),
        "sgu_w": nrm(ks[10], (N_EVEN, N_A_GROUPS, CHUNK, CHUNK), CHUNK ** -0.5),
        "sgu_b": 1.0 + nrm(ks[11], (N_EVEN, N_A_GROUPS, CHUNK), 0.02),
        "conv_w": nrm(ks[12], (N_EVEN, CONV_WIDTH, D_B), CONV_WIDTH ** -0.5),
        "conv_b": nrm(ks[13], (N_EVEN, D_B), 0.01),
        "conv_ln_g": 1.0 + nrm(ks[14], (N_EVEN, D_B), 0.02),
        "conv_ln_b": nrm(ks[15], (N_EVEN, D_B), 0.01),
        "ab_w_out": nrm(ks[16], (N_EVEN, D_A + D_B, d), (D_A + D_B) ** -0.5),
        "mla_w_in": nrm(ks[17], (N_ODD, d, D_MLA_IN), d ** -0.5),
        "mla_q_norm_g": 1.0 + nrm(ks[18], (N_ODD, Q_RANK), 0.02),
        "mla_kv_norm_g": 1.0 + nrm(ks[19], (N_ODD, KV_RANK), 0.02),
        "mla_w_uq": nrm(ks[20], (N_ODD, Q_RANK, N_HEADS * QK_DIM), Q_RANK ** -0.5),
        "mla_w_ukv": nrm(ks[21], (N_ODD, KV_RANK, N_HEADS * (NOPE_DIM + V_DIM)), KV_RANK ** -0.5),
        "mla_q_head_g": 1.0 + nrm(ks[22], (N_ODD, QK_DIM), 0.02),
        "mla_k_head_g": 1.0 + nrm(ks[23], (N_ODD, QK_DIM), 0.02),
        "mla_w_out": nrm(ks[24], (N_ODD, N_HEADS * V_DIM, d), (N_HEADS * V_DIM) ** -0.5),
    }


def reference(x, c, norm1_g, norm2_g, ada_w, ada_b, mlp_w1, mlp_w2,
              ab_w_in, sgu_norm_g, sgu_w, sgu_b, conv_w, conv_b, conv_ln_g, conv_ln_b, ab_w_out,
              mla_w_in, mla_q_norm_g, mla_kv_norm_g, mla_w_uq, mla_w_ukv,
              mla_q_head_g, mla_k_head_g, mla_w_out):
    c_act = jax.nn.silu(c.astype(x.dtype))
    for l in range(DEPTH):
        mod = c_act @ ada_w[l] + ada_b[l]
        shift1, scale1, gate1, shift2, scale2, gate2 = jnp.split(mod, 6, axis=-1)
        h = _modulate(_rms(x, norm1_g[l]), shift1, scale1)
        if l % 2 == 0:
            e = l // 2
            mix = _even_mixer(h, ab_w_in[e], sgu_norm_g[e], sgu_w[e], sgu_b[e], conv_w[e],
                              conv_b[e], conv_ln_g[e], conv_ln_b[e], ab_w_out[e])
        else:
            o = l // 2
            mix = _mla_mixer(h, mla_w_in[o], mla_q_norm_g[o], mla_kv_norm_g[o], mla_w_uq[o],
                             mla_w_ukv[o], mla_q_head_g[o], mla_k_head_g[o], mla_w_out[o])
        x = x + gate1[:, None, :] * mix
        h = _modulate(_rms(x, norm2_g[l]), shift2, scale2)
        x = x + gate2[:, None, :] * (jnp.square(jax.nn.relu(h @ mlp_w1[l])) @ mlp_w2[l])
    return x
```

```python
import functools

import jax
import jax.numpy as jnp
from jax import lax
from jax.experimental import pallas as pl
from jax.experimental.pallas import tpu as pltpu

F32 = jnp.float32
BF16 = jnp.bfloat16

D_MODEL = 2048
D_FF = 4 * D_MODEL
EPS = 1e-6
GROUP_DIM = 128
N_GROUPS = 8
CHUNK = 128
CONV_WIDTH = 31
D_A = D_MODEL // 2
D_B = D_MODEL - D_A
N_HEADS = 16
Q_RANK = 512
KV_RANK = 512
NOPE = 128
ROPE = 64
V_DIM = 128
QK_DIM = NOPE + ROPE
ROPE_THETA = 10000.0
ATTN_SCALE = QK_DIM ** -0.5

LANES = 128
ROWS = 128
HALO = 32
VMEM_CAP = 60 * 1024 * 1024

TM = 512
TM_MLA = 256
TF = 1024
TQ = 512


def _params(semantics, vmem_bytes):
    return pltpu.CompilerParams(dimension_semantics=semantics,
                                vmem_limit_bytes=min(int(vmem_bytes), VMEM_CAP))


def _resident(shape):
    nd = len(shape)
    return pl.BlockSpec(shape, lambda *_: (0,) * nd, pipeline_mode=pl.Buffered(1))


def _sigmoid(x):
    return 1.0 / (1.0 + jnp.exp(-x))


def _gelu_tanh(x):
    return 0.5 * x * (1.0 + jnp.tanh(0.7978845608028654 * (x + 0.044715 * (x * x * x))))


def _norm_mod(xb, g, shift, scale1p):
    ms = jnp.mean(xb * xb, axis=-1, keepdims=True)
    y = xb * lax.rsqrt(ms + EPS)
    return (y * g) * scale1p + shift


def _ada_kernel(c_ref, w_ref, b_ref, o_ref):
    c = c_ref[...]
    ca = (c * _sigmoid(c)).astype(BF16)
    o_ref[0] = jnp.dot(ca, w_ref[0].astype(BF16), preferred_element_type=F32) + b_ref[0]


def _ada(c_pad, ada_w, ada_b):
    depth, d, n = ada_w.shape
    tn = 1024
    rows = c_pad.shape[0]
    return pl.pallas_call(
        _ada_kernel,
        grid=(depth, n // tn),
        in_specs=[
            pl.BlockSpec((rows, d), lambda l, j: (0, 0)),
            pl.BlockSpec((1, d, tn), lambda l, j: (l, 0, j)),
            pl.BlockSpec((1, 1, tn), lambda l, j: (l, 0, j)),
        ],
        out_specs=pl.BlockSpec((1, rows, tn), lambda l, j: (l, 0, j)),
        out_shape=jax.ShapeDtypeStruct((depth, rows, n), F32),
        compiler_params=_params(("arbitrary", "arbitrary"), 2 * d * tn * 4 + 8 * d * tn + (4 << 20)),
        name="ada_mod",
    )(c_pad, ada_w, ada_b.reshape(depth, 1, n))


def _even_kernel(x_ref, mod_ref, g1_ref, win_ref, sng_ref, sw_ref, sbt_ref, cw_ref, cb_ref, lg_ref, lb_ref,
                 out_ref, h_ref, proj_ref, ybuf_ref, conv_ref):
    tm = x_ref.shape[0]
    nchunk = tm // ROWS
    shift = mod_ref[0:1, :]
    scale1p = 1.0 + mod_ref[1:2, :]
    g1 = g1_ref[...]

    def norm_body(c, carry):
        r0 = pl.multiple_of(c * ROWS, ROWS)
        h_ref[pl.ds(r0, ROWS), :] = _norm_mod(x_ref[pl.ds(r0, ROWS), :], g1, shift, scale1p).astype(BF16)
        return carry

    lax.fori_loop(0, nchunk, norm_body, 0)
    proj_ref[...] = jnp.dot(h_ref[...], win_ref[...], preferred_element_type=F32)

    @pl.when(pl.program_id(1) == 0)
    def _():
        ybuf_ref[0:HALO, :] = jnp.zeros((HALO, D_B), F32)

    row = lax.broadcasted_iota(jnp.int32, (CHUNK, CHUNK), 0)
    col = lax.broadcasted_iota(jnp.int32, (CHUNK, CHUNK), 1)
    causal = col <= row

    def sgu_body(c, carry):
        r0 = pl.multiple_of(c * ROWS, ROWS)
        for g in range(N_GROUPS):
            lo, hi = g * GROUP_DIM, (g + 1) * GROUP_DIM
            u = _gelu_tanh(proj_ref[pl.ds(r0, ROWS), lo:hi])
            v = _gelu_tanh(proj_ref[pl.ds(r0, ROWS), D_A + lo:D_A + hi])
            vn = v * lax.rsqrt(jnp.mean(v * v, axis=-1, keepdims=True) + EPS) * sng_ref[0:1, lo:hi]
            w = jnp.where(causal, sw_ref[g], 0.0).astype(BF16)
            mixed = jnp.dot(w, vn.astype(BF16), preferred_element_type=F32) + sbt_ref[:, g:g + 1]
            out_ref[pl.ds(r0, ROWS), lo:hi] = (u * mixed).astype(BF16)
        a = proj_ref[pl.ds(r0, ROWS), 2 * D_A:2 * D_A + D_B]
        gate = proj_ref[pl.ds(r0, ROWS), 2 * D_A + D_B:2 * D_A + 2 * D_B]
        ybuf_ref[pl.ds(HALO + r0, ROWS), :] = a * _sigmoid(gate)
        return carry

    lax.fori_loop(0, nchunk, sgu_body, 0)

    base = HALO - (CONV_WIDTH - 1)
    for c in range(nchunk):
        for lc in range(D_B // LANES):
            l0, l1 = lc * LANES, (lc + 1) * LANES
            acc = jnp.broadcast_to(cb_ref[0:1, l0:l1], (ROWS, LANES))
            for k in range(CONV_WIDTH):
                s0 = c * ROWS + base + k
                acc = acc + ybuf_ref[s0:s0 + ROWS, l0:l1] * cw_ref[k:k + 1, l0:l1]
            conv_ref[c * ROWS:(c + 1) * ROWS, l0:l1] = acc
    ybuf_ref[0:HALO, :] = ybuf_ref[tm:tm + HALO, :]

    lg = lg_ref[...]
    lb = lb_ref[...]

    def ln_body(c, carry):
        r0 = pl.multiple_of(c * ROWS, ROWS)
        y = conv_ref[pl.ds(r0, ROWS), :]
        mu = jnp.mean(y, axis=-1, keepdims=True)
        d = y - mu
        var = jnp.mean(d * d, axis=-1, keepdims=True)
        z = (d * lax.rsqrt(var + EPS)) * lg + lb
        out_ref[pl.ds(r0, ROWS), D_A:D_A + D_B] = (z * _sigmoid(z)).astype(BF16)
        return carry

    lax.fori_loop(0, nchunk, ln_body, 0)


def _even_mixer_pre(x, mod, g1, w_in, sng, sgu_w, sgu_bt, conv_w, conv_b, ln_g, ln_b):
    b, s, d = x.shape
    n_in = w_in.shape[1]
    vmem = (2 * TM * d * 4 + d * n_in * 2 + 2 * TM * d * 2 + TM * n_in * 4 + TM * d * 2
            + (TM + HALO) * D_B * 4 + TM * D_B * 4 + (6 << 20))
    return pl.pallas_call(
        _even_kernel,
        grid=(b, s // TM),
        in_specs=[
            pl.BlockSpec((None, TM, d), lambda i, m: (i, m, 0)),
            pl.BlockSpec((None, 6, d), lambda i, m: (i, 0, 0)),
            _resident((1, d)),
            _resident((d, n_in)),
            _resident((1, D_A)),
            _resident((N_GROUPS, CHUNK, CHUNK)),
            _resident((CHUNK, N_GROUPS)),
            _resident((CONV_WIDTH, D_B)),
            _resident((1, D_B)),
            _resident((1, D_B)),
            _resident((1, D_B)),
        ],
        out_specs=pl.BlockSpec((None, TM, d), lambda i, m: (i, m, 0)),
        out_shape=jax.ShapeDtypeStruct((b, s, d), BF16),
        scratch_shapes=[
            pltpu.VMEM((TM, d), BF16),
            pltpu.VMEM((TM, n_in), F32),
            pltpu.VMEM((TM + HALO, D_B), F32),
            pltpu.VMEM((TM, D_B), F32),
        ],
        compiler_params=_params(("arbitrary", "arbitrary"), vmem),
        name="even_mixer",
    )(x, mod, g1, w_in, sng, sgu_w, sgu_bt, conv_w, conv_b, ln_g, ln_b)


def _mla_pre_kernel(x_ref, mod_ref, g1_ref, win_ref, qg_ref, kvg_ref, wuq_ref, wuk_ref, wuv_ref,
                    qhg_ref, khg_ref, cos_ref, sin_ref,
                    qn_ref, qr_ref, kn_ref, kr_ref, v_ref,
                    h_ref, proj_ref, cq_ref, ckv_ref, qbig_ref):
    tm = x_ref.shape[0]
    nchunk = tm // ROWS
    shift = mod_ref[0:1, :]
    scale1p = 1.0 + mod_ref[1:2, :]
    g1 = g1_ref[...]

    def norm_body(c, carry):
        r0 = pl.multiple_of(c * ROWS, ROWS)
        h_ref[pl.ds(r0, ROWS), :] = _norm_mod(x_ref[pl.ds(r0, ROWS), :], g1, shift, scale1p).astype(BF16)
        return carry

    lax.fori_loop(0, nchunk, norm_body, 0)
    proj_ref[...] = jnp.dot(h_ref[...], win_ref[...], preferred_element_type=F32)

    qg = qg_ref[...]
    kvg = kvg_ref[...]
    lane = lax.broadcasted_iota(jnp.int32, (ROWS, LANES), 1)
    low = lane < ROPE
    kg_nope = khg_ref[0:1, :]
    kg_lo, kg_lo_sw = khg_ref[1:2, :], khg_ref[2:3, :]
    kg_hi, kg_hi_sw = khg_ref[3:4, :], khg_ref[4:5, :]

    def lat_body(c, carry):
        r0 = pl.multiple_of(c * ROWS, ROWS)
        cq = proj_ref[pl.ds(r0, ROWS), 0:Q_RANK]
        cq = cq * lax.rsqrt(jnp.mean(cq * cq, axis=-1, keepdims=True) + EPS) * qg
        cq_ref[pl.ds(r0, ROWS), :] = cq.astype(BF16)
        ckv = proj_ref[pl.ds(r0, ROWS), Q_RANK:Q_RANK + KV_RANK]
        ckv = ckv * lax.rsqrt(jnp.mean(ckv * ckv, axis=-1, keepdims=True) + EPS) * kvg
        ckv_ref[pl.ds(r0, ROWS), :] = ckv.astype(BF16)
        cos = cos_ref[pl.ds(r0, ROWS), :]
        sin = sin_ref[pl.ds(r0, ROWS), :]
        o = Q_RANK + KV_RANK
        for j, (ga, gb) in enumerate(((kg_lo, kg_lo_sw), (kg_hi, kg_hi_sw))):
            kr = proj_ref[pl.ds(r0, ROWS), o + 2 * j * LANES:o + (2 * j + 1) * LANES]
            krs = proj_ref[pl.ds(r0, ROWS), o + (2 * j + 1) * LANES:o + (2 * j + 2) * LANES]
            inv = lax.rsqrt(jnp.sum(kr * kr, axis=-1, keepdims=True) * (1.0 / ROPE) + EPS)
            kr_ref[j, pl.ds(r0, ROWS), :] = ((kr * inv * ga) * cos + (krs * inv * gb) * sin).astype(BF16)
        return carry

    lax.fori_loop(0, nchunk, lat_body, 0)

    qbig_ref[...] = jnp.dot(cq_ref[...], wuq_ref[...], preferred_element_type=F32)
    qg_nope = qhg_ref[0:1, :]
    qg_r, qg_r_sw = qhg_ref[1:2, :], qhg_ref[2:3, :]
    n_nope = N_HEADS * NOPE
    n_rope = N_HEADS * ROPE

    def q_body(c, carry):
        r0 = pl.multiple_of(c * ROWS, ROWS)
        for hd in range(N_HEADS):
            q = qbig_ref[pl.ds(r0, ROWS), hd * NOPE:(hd + 1) * NOPE]
            inv = lax.rsqrt(jnp.mean(q * q, axis=-1, keepdims=True) + EPS)
            qn_ref[pl.ds(r0, ROWS), hd * NOPE:(hd + 1) * NOPE] = ((q * inv * qg_nope) * ATTN_SCALE).astype(BF16)
        cos = cos_ref[pl.ds(r0, ROWS), :]
        sin = sin_ref[pl.ds(r0, ROWS), :]
        for p in range(n_rope // LANES):
            q = qbig_ref[pl.ds(r0, ROWS), n_nope + p * LANES:n_nope + (p + 1) * LANES]
            qs = qbig_ref[pl.ds(r0, ROWS), n_nope + n_rope + p * LANES:n_nope + n_rope + (p + 1) * LANES]
            sq = q * q
            s_lo = jnp.sum(jnp.where(low, sq, 0.0), axis=-1, keepdims=True)
            s_hi = jnp.sum(jnp.where(low, 0.0, sq), axis=-1, keepdims=True)
            inv = lax.rsqrt(jnp.where(low, s_lo, s_hi) * (1.0 / ROPE) + EPS)
            r = (q * inv * qg_r) * cos + (qs * inv * qg_r_sw) * sin
            qr_ref[pl.ds(r0, ROWS), p * LANES:(p + 1) * LANES] = (r * ATTN_SCALE).astype(BF16)
        return carry

    lax.fori_loop(0, nchunk, q_body, 0)

    qbig_ref[:, 0:n_nope] = jnp.dot(ckv_ref[...], wuk_ref[...], preferred_element_type=F32)

    def k_body(c, carry):
        r0 = pl.multiple_of(c * ROWS, ROWS)
        for hd in range(N_HEADS):
            k = qbig_ref[pl.ds(r0, ROWS), hd * NOPE:(hd + 1) * NOPE]
            inv = lax.rsqrt(jnp.mean(k * k, axis=-1, keepdims=True) + EPS)
            kn_ref[pl.ds(r0, ROWS), hd * NOPE:(hd + 1) * NOPE] = (k * inv * kg_nope).astype(BF16)
        return carry

    lax.fori_loop(0, nchunk, k_body, 0)
    v_ref[...] = jnp.dot(ckv_ref[...], wuv_ref[...], preferred_element_type=F32).astype(BF16)


def _mla_pre(x, mod, g1, w_in, qg, kvg, w_uq, w_uk, w_uv, qhg, khg, cos2, sin2):
    b, s, d = x.shape
    n_in = w_in.shape[1]
    n_q = w_uq.shape[1]
    n_nope = N_HEADS * NOPE
    n_rope = N_HEADS * ROPE
    weights = (d * n_in + Q_RANK * n_q + 2 * KV_RANK * n_nope) * 2
    tm = TM_MLA
    vmem = (2 * tm * d * 4 + weights + 2 * tm * (3 * n_nope + n_rope + 2 * LANES) * 2 + 4 * tm * LANES * 4
            + tm * d * 2 + tm * n_in * 4 + 2 * tm * Q_RANK * 2 + tm * n_q * 4 + (8 << 20))
    tile = lambda n: pl.BlockSpec((None, tm, n), lambda i, m: (i, m, 0))
    return pl.pallas_call(
        _mla_pre_kernel,
        grid=(b, s // tm),
        in_specs=[
            tile(d),
            pl.BlockSpec((None, 6, d), lambda i, m: (i, 0, 0)),
            _resident((1, d)),
            _resident((d, n_in)),
            _resident((1, Q_RANK)),
            _resident((1, KV_RANK)),
            _resident((Q_RANK, n_q)),
            _resident((KV_RANK, n_nope)),
            _resident((KV_RANK, n_nope)),
            _resident((8, LANES)),
            _resident((8, LANES)),
            pl.BlockSpec((tm, LANES), lambda i, m: (m, 0)),
            pl.BlockSpec((tm, LANES), lambda i, m: (m, 0)),
        ],
        out_specs=[
            tile(n_nope),
            tile(n_rope),
            tile(n_nope),
            pl.BlockSpec((None, 2, tm, LANES), lambda i, m: (i, 0, m, 0)),
            tile(n_nope),
        ],
        out_shape=[
            jax.ShapeDtypeStruct((b, s, n_nope), BF16),
            jax.ShapeDtypeStruct((b, s, n_rope), BF16),
            jax.ShapeDtypeStruct((b, s, n_nope), BF16),
            jax.ShapeDtypeStruct((b, 2, s, LANES), BF16),
            jax.ShapeDtypeStruct((b, s, n_nope), BF16),
        ],
        scratch_shapes=[
            pltpu.VMEM((tm, d), BF16),
            pltpu.VMEM((tm, n_in), F32),
            pltpu.VMEM((tm, Q_RANK), BF16),
            pltpu.VMEM((tm, KV_RANK), BF16),
            pltpu.VMEM((tm, n_q), F32),
        ],
        compiler_params=_params(("arbitrary", "arbitrary"), vmem),
        name="mla_proj",
    )(x, mod, g1, w_in, qg, kvg, w_uq, w_uk, w_uv, qhg, khg, cos2, sin2)


def _attn_kernel(qn_ref, qr_ref, kn_ref, kr_ref, v_ref, o_ref):
    tq = qn_ref.shape[0]
    hd = pl.program_id(1)
    qi = pl.program_id(2)
    lane = lax.broadcasted_iota(jnp.int32, (tq, LANES), 1)
    lo = (hd % 2) * ROPE
    keep = jnp.logical_and(lane >= lo, lane < lo + ROPE)
    qr = jnp.where(keep, qr_ref[...], jnp.zeros((), BF16))
    q = jnp.concatenate([qn_ref[...], qr], axis=-1)
    row = lax.broadcasted_iota(jnp.int32, (tq, tq), 0)
    col = lax.broadcasted_iota(jnp.int32, (tq, tq), 1)

    def step(kb, carry, diagonal):
        m, l, acc = carry
        k0 = pl.multiple_of(kb * tq, tq)
        k = jnp.concatenate([kn_ref[pl.ds(k0, tq), :], kr_ref[pl.ds(k0, tq), :]], axis=-1)
        s = lax.dot_general(q, k, (((1,), (1,)), ((), ())), preferred_element_type=F32)
        if diagonal:
            s = jnp.where(col <= row, s, -jnp.inf)
        m_new = jnp.maximum(m, jnp.max(s, axis=-1, keepdims=True))
        alpha = jnp.exp(m - m_new)
        p = jnp.exp(s - m_new)
        l = alpha * l + jnp.sum(p, axis=-1, keepdims=True)
        acc = alpha * acc + jnp.dot(p.astype(BF16), v_ref[pl.ds(k0, tq), :], preferred_element_type=F32)
        return m_new, l, acc

    init = (jnp.full((tq, 1), -jnp.inf, F32), jnp.zeros((tq, 1), F32), jnp.zeros((tq, V_DIM), F32))
    carry = lax.fori_loop(0, qi, lambda kb, cr: step(kb, cr, False), init)
    _, l, acc = step(qi, carry, True)
    o_ref[...] = (acc / l).astype(BF16)


def _attention(qn, qr, kn, kr2, v):
    b, s, _ = qn.shape
    vmem = 6 * s * LANES * 2 + 8 * TQ * LANES * 2 + 8 * TQ * TQ * 4 + (8 << 20)
    return pl.pallas_call(
        _attn_kernel,
        grid=(b, N_HEADS, s // TQ),
        in_specs=[
            pl.BlockSpec((None, TQ, NOPE), lambda i, h, q: (i, q, h)),
            pl.BlockSpec((None, TQ, LANES), lambda i, h, q: (i, q, h // 2)),
            pl.BlockSpec((None, s, NOPE), lambda i, h, q: (i, 0, h)),
            pl.BlockSpec((None, None, s, LANES), lambda i, h, q: (i, h % 2, 0, 0)),
            pl.BlockSpec((None, s, V_DIM), lambda i, h, q: (i, 0, h)),
        ],
        out_specs=pl.BlockSpec((None, TQ, V_DIM), lambda i, h, q: (i, q, h)),
        out_shape=jax.ShapeDtypeStruct((b, s, N_HEADS * V_DIM), BF16),
        compiler_params=_params(("arbitrary", "arbitrary", "arbitrary"), vmem),
        name="mla_attention",
    )(qn, qr, kn, kr2, v)


def _oproj_kernel(x_ref, mix_ref, mod_ref, g2_ref, wo_ref, xn_ref, h_ref):
    tm = x_ref.shape[0]
    gate1 = mod_ref[2:3, :]
    shift2 = mod_ref[3:4, :]
    scale2p = 1.0 + mod_ref[4:5, :]
    g2 = g2_ref[...]
    xn_ref[...] = jnp.dot(mix_ref[...], wo_ref[...], preferred_element_type=F32)

    def body(c, carry):
        r0 = pl.multiple_of(c * ROWS, ROWS)
        xn = x_ref[pl.ds(r0, ROWS), :] + gate1 * xn_ref[pl.ds(r0, ROWS), :]
        xn_ref[pl.ds(r0, ROWS), :] = xn
        h_ref[pl.ds(r0, ROWS), :] = _norm_mod(xn, g2, shift2, scale2p).astype(BF16)
        return carry

    lax.fori_loop(0, tm // ROWS, body, 0)


def _oproj(x, mix, mod, g2, w_out):
    b, s, d = x.shape
    vmem = 2 * TM * d * 4 + 2 * TM * d * 2 + d * d * 2 + 2 * TM * d * 4 + 2 * TM * d * 2 + (6 << 20)
    tile = pl.BlockSpec((None, TM, d), lambda i, m: (i, m, 0))
    return pl.pallas_call(
        _oproj_kernel,
        grid=(b, s // TM),
        in_specs=[tile, tile, pl.BlockSpec((None, 6, d), lambda i, m: (i, 0, 0)), _resident((1, d)),
                  _resident((d, d))],
        out_specs=[tile, tile],
        out_shape=[jax.ShapeDtypeStruct((b, s, d), F32), jax.ShapeDtypeStruct((b, s, d), BF16)],
        compiler_params=_params(("arbitrary", "arbitrary"), vmem),
        name="out_proj",
    )(x, mix, mod, g2, w_out)


def _mlp_kernel(h_ref, x_ref, mod_ref, w1_ref, w2_ref, out_ref, acc_ref):
    j = pl.program_id(2)

    @pl.when(j == 0)
    def _():
        acc_ref[...] = jnp.zeros(acc_ref.shape, F32)

    a = jnp.dot(h_ref[...], w1_ref[...], preferred_element_type=F32)
    r = jnp.square(jnp.maximum(a, 0.0)).astype(BF16)
    acc_ref[...] += jnp.dot(r, w2_ref[...], preferred_element_type=F32)

    @pl.when(j == pl.num_programs(2) - 1)
    def _():
        out_ref[...] = x_ref[...] + mod_ref[5:6, :] * acc_ref[...]


def _mlp(h, x, mod, w1, w2):
    b, s, d = x.shape
    dff = w1.shape[1]
    vmem = (2 * TM * d * 2 + 2 * TM * d * 4 + 4 * d * TF * 2 + 2 * TM * d * 4 + TM * d * 4
            + TM * TF * 6 + TM * d * 4 + (4 << 20))
    tile = pl.BlockSpec((None, TM, d), lambda i, m, j: (i, m, 0))
    return pl.pallas_call(
        _mlp_kernel,
        grid=(b, s // TM, dff // TF),
        in_specs=[
            tile,
            tile,
            pl.BlockSpec((None, 6, d), lambda i, m, j: (i, 0, 0)),
            pl.BlockSpec((d, TF), lambda i, m, j: (0, j)),
            pl.BlockSpec((TF, d), lambda i, m, j: (j, 0)),
        ],
        out_specs=tile,
        out_shape=jax.ShapeDtypeStruct((b, s, d), F32),
        scratch_shapes=[pltpu.VMEM((TM, d), F32)],
        compiler_params=_params(("arbitrary", "arbitrary", "arbitrary"), vmem),
        name="relu2_mlp",
    )(h, x, mod, w1, w2)


def _rope_tables(s):
    pos = jnp.arange(s, dtype=F32)
    inv = ROPE_THETA ** (-jnp.arange(0, ROPE, 2, dtype=F32) / ROPE)
    ang = pos[:, None] * inv[None, :]
    cos, sin = jnp.cos(ang), jnp.sin(ang)
    return jnp.tile(cos, (1, 4)), jnp.tile(sin, (1, 4))


def _swap_halves(w):
    half = ROPE // 2
    return jnp.concatenate([-w[..., half:], w[..., :half]], axis=-1)


def _swap_gain(g):
    half = ROPE // 2
    return jnp.concatenate([g[half:], g[:half]])


def _rows8(rows):
    rows = [jnp.asarray(r, F32) for r in rows]
    rows += [jnp.zeros((LANES,), F32)] * (8 - len(rows))
    return jnp.stack(rows)


def _prep_mla(w_in, w_uq, w_ukv, qhg, khg):
    d = w_in.shape[0]
    zeros = jnp.zeros((d, ROPE), F32)
    w_kr = w_in[:, Q_RANK + KV_RANK:]
    w_krs = _swap_halves(w_kr)
    w_in2 = jnp.concatenate([
        w_in[:, :Q_RANK + KV_RANK],
        w_kr, zeros, w_krs, zeros,
        zeros, w_kr, zeros, w_krs,
    ], axis=1).astype(BF16)
    uq = w_uq.reshape(Q_RANK, N_HEADS, QK_DIM)
    uq_rope = uq[:, :, NOPE:]
    w_uq2 = jnp.concatenate([
        uq[:, :, :NOPE].reshape(Q_RANK, N_HEADS * NOPE),
        uq_rope.reshape(Q_RANK, N_HEADS * ROPE),
        _swap_halves(uq_rope).reshape(Q_RANK, N_HEADS * ROPE),
    ], axis=1).astype(BF16)
    ukv = w_ukv.reshape(KV_RANK, N_HEADS, NOPE + V_DIM)
    w_uk = ukv[:, :, :NOPE].reshape(KV_RANK, N_HEADS * NOPE).astype(BF16)
    w_uv = ukv[:, :, NOPE:].reshape(KV_RANK, N_HEADS * V_DIM).astype(BF16)
    z = jnp.zeros((ROPE,), F32)
    qr, kr = qhg[NOPE:], khg[NOPE:]
    qhg2 = _rows8([qhg[:NOPE], jnp.tile(qr, 2), jnp.tile(_swap_gain(qr), 2)])
    khg2 = _rows8([khg[:NOPE],
                   jnp.concatenate([kr, z]), jnp.concatenate([_swap_gain(kr), z]),
                   jnp.concatenate([z, kr]), jnp.concatenate([z, _swap_gain(kr)])])
    return w_in2, w_uq2, w_uk, w_uv, qhg2, khg2


def kernel(x, c, norm1_g, norm2_g, ada_w, ada_b, mlp_w1, mlp_w2, ab_w_in, sgu_norm_g, sgu_w, sgu_b, conv_w, conv_b,
           conv_ln_g, conv_ln_b, ab_w_out, mla_w_in, mla_q_norm_g, mla_kv_norm_g, mla_w_uq, mla_w_ukv,
           mla_q_head_g, mla_k_head_g, mla_w_out):
    b, s, d = x.shape
    depth = ada_w.shape[0]
    c_pad = jnp.pad(c.astype(F32), ((0, 8 - b), (0, 0)))
    mod = _ada(c_pad, ada_w, ada_b)[:, :b].reshape(depth, b, 6, d)
    cos2, sin2 = _rope_tables(s)

    for l in range(depth):
        g1 = norm1_g[l].reshape(1, d)
        g2 = norm2_g[l].reshape(1, d)
        if l % 2 == 0:
            e = l // 2
            mix = _even_mixer_pre(
                x, mod[l], g1, ab_w_in[e].astype(BF16), sgu_norm_g[e].reshape(1, D_A), sgu_w[e], sgu_b[e].T,
                conv_w[e], conv_b[e].reshape(1, D_B), conv_ln_g[e].reshape(1, D_B), conv_ln_b[e].reshape(1, D_B))
            w_out = ab_w_out[e]
        else:
            o = l // 2
            w_in2, w_uq2, w_uk, w_uv, qhg2, khg2 = _prep_mla(
                mla_w_in[o], mla_w_uq[o], mla_w_ukv[o], mla_q_head_g[o], mla_k_head_g[o])
            qn, qr, kn, kr2, v = _mla_pre(
                x, mod[l], g1, w_in2, mla_q_norm_g[o].reshape(1, Q_RANK), mla_kv_norm_g[o].reshape(1, KV_RANK),
                w_uq2, w_uk, w_uv, qhg2, khg2, cos2, sin2)
            mix = _attention(qn, qr, kn, kr2, v)
            w_out = mla_w_out[o]
        xn, h2 = _oproj(x, mix, mod[l], g2, w_out.astype(BF16))
        x = _mlp(h2, xn, mod[l], mlp_w1[l].astype(BF16), mlp_w2[l].astype(BF16))
    return x
```

```python
import jax
import jax.numpy as jnp
from jax import lax
from jax.experimental import pallas as pl
from jax.experimental.pallas import tpu as pltpu

F32 = jnp.float32
BF16 = jnp.bfloat16

D_MODEL = 2048
D_FF = 4 * D_MODEL
EPS = 1e-6
GROUP_DIM = 128
N_GROUPS = 8
CHUNK = 128
CONV_WIDTH = 31
D_A = D_MODEL // 2
D_B = D_MODEL - D_A
N_HEADS = 16
Q_RANK = 512
KV_RANK = 512
NOPE = 128
ROPE = 64
V_DIM = 128
QK_DIM = NOPE + ROPE
ROPE_THETA = 10000.0
ATTN_SCALE = QK_DIM ** -0.5

LANES = 128
SUBLANES = 8
ROWS = 128
HALO = 32
VMEM_CAP = 60 * 1024 * 1024

TM = 512
TM_MLA = 256
TF = 1024
TQ = 512


def _params(semantics, vmem_bytes):
    return pltpu.CompilerParams(dimension_semantics=semantics,
                                vmem_limit_bytes=min(int(vmem_bytes), VMEM_CAP))


def _resident(shape):
    nd = len(shape)
    return pl.BlockSpec(shape, lambda *_: (0,) * nd, pipeline_mode=pl.Buffered(1))


def _sigmoid(x):
    return 1.0 / (1.0 + jnp.exp(-x))


def _gelu_tanh(x):
    return 0.5 * x * (1.0 + jnp.tanh(0.7978845608028654 * (x + 0.044715 * (x * x * x))))


def _norm_mod(xb, g, shift, scale1p):
    ms = jnp.mean(xb * xb, axis=-1, keepdims=True)
    y = xb * lax.rsqrt(ms + EPS)
    return (y * g) * scale1p + shift


def _ada_kernel(c_ref, w_ref, b_ref, o_ref):
    c = c_ref[...]
    ca = (c * _sigmoid(c)).astype(BF16)
    o_ref[0] = jnp.dot(ca, w_ref[0].astype(BF16), preferred_element_type=F32) + b_ref[0]


def _ada(c_pad, ada_w, ada_b):
    depth, d, n = ada_w.shape
    tn = 1024
    rows = c_pad.shape[0]
    return pl.pallas_call(
        _ada_kernel,
        grid=(depth, n // tn),
        in_specs=[
            pl.BlockSpec((rows, d), lambda l, j: (0, 0)),
            pl.BlockSpec((1, d, tn), lambda l, j: (l, 0, j)),
            pl.BlockSpec((1, 1, tn), lambda l, j: (l, 0, j)),
        ],
        out_specs=pl.BlockSpec((1, rows, tn), lambda l, j: (l, 0, j)),
        out_shape=jax.ShapeDtypeStruct((depth, rows, n), F32),
        compiler_params=_params(("arbitrary", "arbitrary"), 2 * d * tn * 4 + 8 * d * tn + (4 << 20)),
        name="ada_mod",
    )(c_pad, ada_w, ada_b.reshape(depth, 1, n))


def _even_kernel(x_ref, mod_ref, g1_ref, win_ref, sng_ref, sw_ref, sbt_ref, cw_ref, cb_ref, lg_ref, lb_ref,
                 out_ref, h_ref, proj_ref, ybuf_ref, conv_ref):
    tm = x_ref.shape[0]
    nchunk = tm // ROWS
    shift = mod_ref[0:1, :]
    scale1p = 1.0 + mod_ref[1:2, :]
    g1 = g1_ref[...]

    def norm_body(c, carry):
        r0 = pl.multiple_of(c * ROWS, ROWS)
        h_ref[pl.ds(r0, ROWS), :] = _norm_mod(x_ref[pl.ds(r0, ROWS), :], g1, shift, scale1p).astype(BF16)
        return carry

    lax.fori_loop(0, nchunk, norm_body, 0)

    @pl.when(pl.program_id(1) == 0)
    def _():
        ybuf_ref[:, 0:HALO, :] = jnp.zeros((N_GROUPS, HALO, LANES), F32)

    row = lax.broadcasted_iota(jnp.int32, (CHUNK, CHUNK), 0)
    col = lax.broadcasted_iota(jnp.int32, (CHUNK, CHUNK), 1)
    causal = col <= row
    base = HALO - (CONV_WIDTH - 1)

    for j in range(N_GROUPS):
        lo, hi = j * LANES, (j + 1) * LANES
        proj_ref[j] = jnp.dot(h_ref[...], win_ref[j], preferred_element_type=F32)
        w = jnp.where(causal, sw_ref[j], 0.0).astype(BF16)
        for c in range(nchunk):
            r0 = c * ROWS
            u = _gelu_tanh(proj_ref[j, r0:r0 + ROWS, 0:LANES])
            v = _gelu_tanh(proj_ref[j, r0:r0 + ROWS, LANES:2 * LANES])
            vn = v * lax.rsqrt(jnp.mean(v * v, axis=-1, keepdims=True) + EPS) * sng_ref[0:1, lo:hi]
            mixed = jnp.dot(w, vn.astype(BF16), preferred_element_type=F32) + sbt_ref[:, j:j + 1]
            out_ref[r0:r0 + ROWS, lo:hi] = (u * mixed).astype(BF16)
            a = proj_ref[j, r0:r0 + ROWS, 2 * LANES:3 * LANES]
            gate = proj_ref[j, r0:r0 + ROWS, 3 * LANES:4 * LANES]
            ybuf_ref[j, HALO + r0:HALO + r0 + ROWS, :] = a * _sigmoid(gate)
        for c in range(nchunk):
            acc = jnp.broadcast_to(cb_ref[0:1, lo:hi], (ROWS, LANES))
            for r in range(SUBLANES):
                ext = SUBLANES if r else 0
                part = None
                for k in range(CONV_WIDTH):
                    if (base + k) % SUBLANES != r:
                        continue
                    o = c * ROWS + SUBLANES * ((base + k) // SUBLANES)
                    term = ybuf_ref[j, o:o + ROWS + ext, :] * cw_ref[k:k + 1, lo:hi]
                    part = term if part is None else part + term
                acc = acc + part[r:r + ROWS]
            conv_ref[c * ROWS:(c + 1) * ROWS, lo:hi] = acc
        ybuf_ref[j, 0:HALO, :] = ybuf_ref[j, tm:tm + HALO, :]

    lg = lg_ref[...]
    lb = lb_ref[...]

    def ln_body(c, carry):
        r0 = pl.multiple_of(c * ROWS, ROWS)
        y = conv_ref[pl.ds(r0, ROWS), :]
        mu = jnp.mean(y, axis=-1, keepdims=True)
        d = y - mu
        var = jnp.mean(d * d, axis=-1, keepdims=True)
        z = (d * lax.rsqrt(var + EPS)) * lg + lb
        out_ref[pl.ds(r0, ROWS), D_A:D_A + D_B] = (z * _sigmoid(z)).astype(BF16)
        return carry

    lax.fori_loop(0, nchunk, ln_body, 0)


def _even_mixer_pre(x, mod, g1, w_in, sng, sgu_w, sgu_bt, conv_w, conv_b, ln_g, ln_b):
    b, s, d = x.shape
    n_in = w_in.shape[0] * w_in.shape[2]
    vmem = (2 * TM * d * 4 + d * n_in * 2 + 2 * TM * d * 2 + TM * n_in * 4 + TM * d * 2
            + (TM + HALO) * D_B * 4 + TM * D_B * 4 + (6 << 20))
    return pl.pallas_call(
        _even_kernel,
        grid=(b, s // TM),
        in_specs=[
            pl.BlockSpec((None, TM, d), lambda i, m: (i, m, 0)),
            pl.BlockSpec((None, 6, d), lambda i, m: (i, 0, 0)),
            _resident((1, d)),
            _resident(w_in.shape),
            _resident((1, D_A)),
            _resident((N_GROUPS, CHUNK, CHUNK)),
            _resident((CHUNK, N_GROUPS)),
            _resident((CONV_WIDTH, D_B)),
            _resident((1, D_B)),
            _resident((1, D_B)),
            _resident((1, D_B)),
        ],
        out_specs=pl.BlockSpec((None, TM, d), lambda i, m: (i, m, 0)),
        out_shape=jax.ShapeDtypeStruct((b, s, d), BF16),
        scratch_shapes=[
            pltpu.VMEM((TM, d), BF16),
            pltpu.VMEM((N_GROUPS, TM, 4 * LANES), F32),
            pltpu.VMEM((N_GROUPS, TM + HALO, LANES), F32),
            pltpu.VMEM((TM, D_B), F32),
        ],
        compiler_params=_params(("arbitrary", "arbitrary"), vmem),
        name="even_mixer",
    )(x, mod, g1, w_in, sng, sgu_w, sgu_bt, conv_w, conv_b, ln_g, ln_b)


def _group_in_proj(w_in):
    d = w_in.shape[0]
    parts = w_in.reshape(d, 4, N_GROUPS, LANES)
    return jnp.transpose(parts, (2, 0, 1, 3)).reshape(N_GROUPS, d, 4 * LANES)


def _mla_pre_kernel(x_ref, mod_ref, g1_ref, win_ref, qg_ref, kvg_ref, wuq_ref, wuk_ref, wuv_ref,
                    qhg_ref, khg_ref, cos_ref, sin_ref,
                    qn_ref, qr_ref, kn_ref, kr_ref, v_ref,
                    h_ref, proj_ref, cq_ref, ckv_ref, qbig_ref, kbig_ref):
    tm = x_ref.shape[0]
    nchunk = tm // ROWS
    shift = mod_ref[0:1, :]
    scale1p = 1.0 + mod_ref[1:2, :]
    g1 = g1_ref[...]

    def norm_body(c, carry):
        r0 = pl.multiple_of(c * ROWS, ROWS)
        h_ref[pl.ds(r0, ROWS), :] = _norm_mod(x_ref[pl.ds(r0, ROWS), :], g1, shift, scale1p).astype(BF16)
        return carry

    lax.fori_loop(0, nchunk, norm_body, 0)
    proj_ref[...] = jnp.dot(h_ref[...], win_ref[...], preferred_element_type=F32)

    qg = qg_ref[...]
    kvg = kvg_ref[...]
    lane = lax.broadcasted_iota(jnp.int32, (ROWS, LANES), 1)
    low = lane < ROPE
    kg_nope = khg_ref[0:1, :]
    kg_lo, kg_lo_sw = khg_ref[1:2, :], khg_ref[2:3, :]
    kg_hi, kg_hi_sw = khg_ref[3:4, :], khg_ref[4:5, :]

    for c in range(nchunk):
        r0 = c * ROWS
        cq = proj_ref[r0:r0 + ROWS, 0:Q_RANK]
        cq = cq * lax.rsqrt(jnp.mean(cq * cq, axis=-1, keepdims=True) + EPS) * qg
        cq_ref[r0:r0 + ROWS, :] = cq.astype(BF16)
        ckv = proj_ref[r0:r0 + ROWS, Q_RANK:Q_RANK + KV_RANK]
        ckv = ckv * lax.rsqrt(jnp.mean(ckv * ckv, axis=-1, keepdims=True) + EPS) * kvg
        ckv_ref[r0:r0 + ROWS, :] = ckv.astype(BF16)
        cos = cos_ref[r0:r0 + ROWS, :]
        sin = sin_ref[r0:r0 + ROWS, :]
        o = Q_RANK + KV_RANK
        for j, (ga, gb) in enumerate(((kg_lo, kg_lo_sw), (kg_hi, kg_hi_sw))):
            kr = proj_ref[r0:r0 + ROWS, o + 2 * j * LANES:o + (2 * j + 1) * LANES]
            krs = proj_ref[r0:r0 + ROWS, o + (2 * j + 1) * LANES:o + (2 * j + 2) * LANES]
            inv = lax.rsqrt(jnp.sum(kr * kr, axis=-1, keepdims=True) * (1.0 / ROPE) + EPS)
            kr_ref[j, r0:r0 + ROWS, :] = ((kr * inv * ga) * cos + (krs * inv * gb) * sin).astype(BF16)

    n_nope = N_HEADS * NOPE
    n_rope = N_HEADS * ROPE
    qbig_ref[...] = jnp.dot(cq_ref[...], wuq_ref[...], preferred_element_type=F32)
    kbig_ref[...] = jnp.dot(ckv_ref[...], wuk_ref[...], preferred_element_type=F32)
    vt = lax.dot_general(wuv_ref[...], ckv_ref[...], (((1,), (1,)), ((), ())), preferred_element_type=F32)
    for hd in range(N_HEADS):
        v_ref[hd] = vt[hd * V_DIM:(hd + 1) * V_DIM, :].astype(BF16)

    qg_nope = qhg_ref[0:1, :]
    qg_r, qg_r_sw = qhg_ref[1:2, :], qhg_ref[2:3, :]
    for c in range(nchunk):
        r0 = c * ROWS
        for hd in range(N_HEADS):
            q = qbig_ref[r0:r0 + ROWS, hd * NOPE:(hd + 1) * NOPE]
            inv = lax.rsqrt(jnp.mean(q * q, axis=-1, keepdims=True) + EPS)
            qn_ref[r0:r0 + ROWS, hd * NOPE:(hd + 1) * NOPE] = ((q * inv * qg_nope) * ATTN_SCALE).astype(BF16)
        cos = cos_ref[r0:r0 + ROWS, :]
        sin = sin_ref[r0:r0 + ROWS, :]
        for p in range(n_rope // LANES):
            q = qbig_ref[r0:r0 + ROWS, n_nope + p * LANES:n_nope + (p + 1) * LANES]
            qs = qbig_ref[r0:r0 + ROWS, n_nope + n_rope + p * LANES:n_nope + n_rope + (p + 1) * LANES]
            sq = q * q
            s_lo = jnp.sum(jnp.where(low, sq, 0.0), axis=-1, keepdims=True)
            s_hi = jnp.sum(jnp.where(low, 0.0, sq), axis=-1, keepdims=True)
            inv = lax.rsqrt(jnp.where(low, s_lo, s_hi) * (1.0 / ROPE) + EPS)
            r = (q * inv * qg_r) * cos + (qs * inv * qg_r_sw) * sin
            qr_ref[r0:r0 + ROWS, p * LANES:(p + 1) * LANES] = (r * ATTN_SCALE).astype(BF16)
        for hd in range(N_HEADS):
            k = kbig_ref[r0:r0 + ROWS, hd * NOPE:(hd + 1) * NOPE]
            inv = lax.rsqrt(jnp.mean(k * k, axis=-1, keepdims=True) + EPS)
            kn_ref[r0:r0 + ROWS, hd * NOPE:(hd + 1) * NOPE] = (k * inv * kg_nope).astype(BF16)


def _mla_pre(x, mod, g1, w_in, qg, kvg, w_uq, w_uk, w_uv, qhg, khg, cos2, sin2):
    b, s, d = x.shape
    n_in = w_in.shape[1]
    n_q = w_uq.shape[1]
    n_nope = N_HEADS * NOPE
    n_rope = N_HEADS * ROPE
    weights = (d * n_in + Q_RANK * n_q + 2 * KV_RANK * n_nope) * 2
    tm = TM_MLA
    vmem = (2 * tm * d * 4 + weights + 2 * tm * (3 * n_nope + n_rope + 2 * LANES) * 2 + 4 * tm * LANES * 4
            + tm * d * 2 + tm * n_in * 4 + 2 * tm * Q_RANK * 2 + tm * (n_q + n_nope) * 4 + (8 << 20))
    tile = lambda n: pl.BlockSpec((None, tm, n), lambda i, m: (i, m, 0))
    return pl.pallas_call(
        _mla_pre_kernel,
        grid=(b, s // tm),
        in_specs=[
            tile(d),
            pl.BlockSpec((None, 6, d), lambda i, m: (i, 0, 0)),
            _resident((1, d)),
            _resident((d, n_in)),
            _resident((1, Q_RANK)),
            _resident((1, KV_RANK)),
            _resident((Q_RANK, n_q)),
            _resident((KV_RANK, n_nope)),
            _resident((n_nope, KV_RANK)),
            _resident((8, LANES)),
            _resident((8, LANES)),
            pl.BlockSpec((tm, LANES), lambda i, m: (m, 0)),
            pl.BlockSpec((tm, LANES), lambda i, m: (m, 0)),
        ],
        out_specs=[
            tile(n_nope),
            tile(n_rope),
            tile(n_nope),
            pl.BlockSpec((None, 2, tm, LANES), lambda i, m: (i, 0, m, 0)),
            pl.BlockSpec((None, N_HEADS, None, V_DIM, tm), lambda i, m: (i, 0, m // (TQ // tm), 0, m % (TQ // tm))),
        ],
        out_shape=[
            jax.ShapeDtypeStruct((b, s, n_nope), BF16),
            jax.ShapeDtypeStruct((b, s, n_rope), BF16),
            jax.ShapeDtypeStruct((b, s, n_nope), BF16),
            jax.ShapeDtypeStruct((b, 2, s, LANES), BF16),
            jax.ShapeDtypeStruct((b, N_HEADS, s // TQ, V_DIM, TQ), BF16),
        ],
        scratch_shapes=[
            pltpu.VMEM((tm, d), BF16),
            pltpu.VMEM((tm, n_in), F32),
            pltpu.VMEM((tm, Q_RANK), BF16),
            pltpu.VMEM((tm, KV_RANK), BF16),
            pltpu.VMEM((tm, n_q), F32),
            pltpu.VMEM((tm, n_nope), F32),
        ],
        compiler_params=_params(("arbitrary", "arbitrary"), vmem),
        name="mla_proj",
    )(x, mod, g1, w_in, qg, kvg, w_uq, w_uk, w_uv, qhg, khg, cos2, sin2)


def _attn_kernel(qn_ref, qr_ref, kn_ref, kr_ref, vt_ref, o_ref):
    tq = qn_ref.shape[0]
    hd = pl.program_id(1)
    qi = pl.program_id(2)
    lane = lax.broadcasted_iota(jnp.int32, (tq, LANES), 1)
    lo = (hd % 2) * ROPE
    keep = jnp.logical_and(lane >= lo, lane < lo + ROPE)
    qr = jnp.where(keep, qr_ref[...], jnp.zeros((), BF16))
    q = jnp.concatenate([qn_ref[...], qr], axis=-1)
    key = lax.broadcasted_iota(jnp.int32, (tq, tq), 0)
    qry = lax.broadcasted_iota(jnp.int32, (tq, tq), 1)

    def step(kb, nblk, carry, diagonal):
        m, l, acc = carry
        k0 = pl.multiple_of(kb * tq, tq)
        rows = nblk * tq
        k = jnp.concatenate([kn_ref[pl.ds(k0, rows), :], kr_ref[pl.ds(k0, rows), :]], axis=-1)
        s = lax.dot_general(k, q, (((1,), (1,)), ((), ())), preferred_element_type=F32)
        if diagonal:
            s = jnp.where(key <= qry, s, -jnp.inf)
        m_new = jnp.maximum(m, jnp.max(s, axis=0, keepdims=True))
        alpha = jnp.exp(m - m_new)
        p = jnp.exp(s - m_new)
        l = alpha * l + jnp.sum(p, axis=0, keepdims=True)
        p = p.astype(BF16)
        acc = alpha * acc
        for i in range(nblk):
            acc = acc + jnp.dot(vt_ref[kb + i], p[i * tq:(i + 1) * tq], preferred_element_type=F32)
        return m_new, l, acc

    init = (jnp.full((1, tq), -jnp.inf, F32), jnp.zeros((1, tq), F32), jnp.zeros((V_DIM, tq), F32))
    carry = lax.fori_loop(0, qi // 2, lambda j, cr: step(2 * j, 2, cr, False), init)
    carry = lax.fori_loop(0, qi % 2, lambda j, cr: step(qi - 1, 1, cr, False), carry)
    _, l, acc = step(qi, 1, carry, True)
    o_ref[...] = jnp.transpose(acc / l).astype(BF16)


def _attention(qn, qr, kn, kr2, vt):
    b, s, _ = qn.shape
    vmem = 6 * s * LANES * 2 + 8 * TQ * LANES * 2 + 12 * TQ * TQ * 4 + (8 << 20)
    return pl.pallas_call(
        _attn_kernel,
        grid=(b, N_HEADS, s // TQ),
        in_specs=[
            pl.BlockSpec((None, TQ, NOPE), lambda i, h, q: (i, q, h)),
            pl.BlockSpec((None, TQ, LANES), lambda i, h, q: (i, q, h // 2)),
            pl.BlockSpec((None, s, NOPE), lambda i, h, q: (i, 0, h)),
            pl.BlockSpec((None, None, s, LANES), lambda i, h, q: (i, h % 2, 0, 0)),
            pl.BlockSpec((None, None, s // TQ, V_DIM, TQ), lambda i, h, q: (i, h, 0, 0, 0)),
        ],
        out_specs=pl.BlockSpec((None, TQ, V_DIM), lambda i, h, q: (i, q, h)),
        out_shape=jax.ShapeDtypeStruct((b, s, N_HEADS * V_DIM), BF16),
        compiler_params=_params(("arbitrary", "arbitrary", "arbitrary"), vmem),
        name="mla_attention",
    )(qn, qr, kn, kr2, vt)


def _oproj_kernel(x_ref, mix_ref, mod_ref, g2_ref, wo_ref, xn_ref, h_ref):
    tm = x_ref.shape[0]
    gate1 = mod_ref[2:3, :]
    shift2 = mod_ref[3:4, :]
    scale2p = 1.0 + mod_ref[4:5, :]
    g2 = g2_ref[...]
    xn_ref[...] = jnp.dot(mix_ref[...], wo_ref[...], preferred_element_type=F32)

    def body(c, carry):
        r0 = pl.multiple_of(c * ROWS, ROWS)
        xn = x_ref[pl.ds(r0, ROWS), :] + gate1 * xn_ref[pl.ds(r0, ROWS), :]
        xn_ref[pl.ds(r0, ROWS), :] = xn
        h_ref[pl.ds(r0, ROWS), :] = _norm_mod(xn, g2, shift2, scale2p).astype(BF16)
        return carry

    lax.fori_loop(0, tm // ROWS, body, 0)


def _oproj(x, mix, mod, g2, w_out):
    b, s, d = x.shape
    vmem = 2 * TM * d * 4 + 2 * TM * d * 2 + d * d * 2 + 2 * TM * d * 4 + 2 * TM * d * 2 + (6 << 20)
    tile = pl.BlockSpec((None, TM, d), lambda i, m: (i, m, 0))
    return pl.pallas_call(
        _oproj_kernel,
        grid=(b, s // TM),
        in_specs=[tile, tile, pl.BlockSpec((None, 6, d), lambda i, m: (i, 0, 0)), _resident((1, d)),
                  _resident((d, d))],
        out_specs=[tile, tile],
        out_shape=[jax.ShapeDtypeStruct((b, s, d), F32), jax.ShapeDtypeStruct((b, s, d), BF16)],
        compiler_params=_params(("arbitrary", "arbitrary"), vmem),
        name="out_proj",
    )(x, mix, mod, g2, w_out)


def _mlp_kernel(h_ref, x_ref, mod_ref, w1_ref, w2_ref, out_ref, acc_ref):
    j = pl.program_id(2)

    @pl.when(j == 0)
    def _():
        acc_ref[...] = jnp.zeros(acc_ref.shape, F32)

    a = jnp.dot(h_ref[...], w1_ref[...], preferred_element_type=F32)
    r = jnp.square(jnp.maximum(a, 0.0)).astype(BF16)
    acc_ref[...] += jnp.dot(r, w2_ref[...], preferred_element_type=F32)

    @pl.when(j == pl.num_programs(2) - 1)
    def _():
        out_ref[...] = x_ref[...] + mod_ref[5:6, :] * acc_ref[...]


def _mlp(h, x, mod, w1, w2):
    b, s, d = x.shape
    dff = w1.shape[1]
    vmem = (2 * TM * d * 2 + 2 * TM * d * 4 + 4 * d * TF * 2 + 2 * TM * d * 4 + TM * d * 4
            + TM * TF * 6 + TM * d * 4 + (4 << 20))
    tile = pl.BlockSpec((None, TM, d), lambda i, m, j: (i, m, 0))
    return pl.pallas_call(
        _mlp_kernel,
        grid=(b, s // TM, dff // TF),
        in_specs=[
            tile,
            tile,
            pl.BlockSpec((None, 6, d), lambda i, m, j: (i, 0, 0)),
            pl.BlockSpec((d, TF), lambda i, m, j: (0, j)),
            pl.BlockSpec((TF, d), lambda i, m, j: (j, 0)),
        ],
        out_specs=tile,
        out_shape=jax.ShapeDtypeStruct((b, s, d), F32),
        scratch_shapes=[pltpu.VMEM((TM, d), F32)],
        compiler_params=_params(("arbitrary", "arbitrary", "arbitrary"), vmem),
        name="relu2_mlp",
    )(h, x, mod, w1, w2)


def _rope_tables(s):
    pos = jnp.arange(s, dtype=F32)
    inv = ROPE_THETA ** (-jnp.arange(0, ROPE, 2, dtype=F32) / ROPE)
    ang = pos[:, None] * inv[None, :]
    cos, sin = jnp.cos(ang), jnp.sin(ang)
    return jnp.tile(cos, (1, 4)), jnp.tile(sin, (1, 4))


def _swap_halves(w):
    half = ROPE // 2
    return jnp.concatenate([-w[..., half:], w[..., :half]], axis=-1)


def _swap_gain(g):
    half = ROPE // 2
    return jnp.concatenate([g[half:], g[:half]])


def _rows8(rows):
    rows = [jnp.asarray(r, F32) for r in rows]
    rows += [jnp.zeros((LANES,), F32)] * (8 - len(rows))
    return jnp.stack(rows)


def _prep_mla(w_in, w_uq, w_ukv, qhg, khg):
    d = w_in.shape[0]
    zeros = jnp.zeros((d, ROPE), F32)
    w_kr = w_in[:, Q_RANK + KV_RANK:]
    w_krs = _swap_halves(w_kr)
    w_in2 = jnp.concatenate([
        w_in[:, :Q_RANK + KV_RANK],
        w_kr, zeros, w_krs, zeros,
        zeros, w_kr, zeros, w_krs,
    ], axis=1).astype(BF16)
    uq = w_uq.reshape(Q_RANK, N_HEADS, QK_DIM)
    uq_rope = uq[:, :, NOPE:]
    w_uq2 = jnp.concatenate([
        uq[:, :, :NOPE].reshape(Q_RANK, N_HEADS * NOPE),
        uq_rope.reshape(Q_RANK, N_HEADS * ROPE),
        _swap_halves(uq_rope).reshape(Q_RANK, N_HEADS * ROPE),
    ], axis=1).astype(BF16)
    ukv = w_ukv.reshape(KV_RANK, N_HEADS, NOPE + V_DIM)
    w_uk = ukv[:, :, :NOPE].reshape(KV_RANK, N_HEADS * NOPE).astype(BF16)
    w_uv = ukv[:, :, NOPE:].reshape(KV_RANK, N_HEADS * V_DIM).T.astype(BF16)
    z = jnp.zeros((ROPE,), F32)
    qr, kr = qhg[NOPE:], khg[NOPE:]
    qhg2 = _rows8([qhg[:NOPE], jnp.tile(qr, 2), jnp.tile(_swap_gain(qr), 2)])
    khg2 = _rows8([khg[:NOPE],
                   jnp.concatenate([kr, z]), jnp.concatenate([_swap_gain(kr), z]),
                   jnp.concatenate([z, kr]), jnp.concatenate([z, _swap_gain(kr)])])
    return w_in2, w_uq2, w_uk, w_uv, qhg2, khg2


def kernel(x, c, norm1_g, norm2_g, ada_w, ada_b, mlp_w1, mlp_w2, ab_w_in, sgu_norm_g, sgu_w, sgu_b, conv_w, conv_b,
           conv_ln_g, conv_ln_b, ab_w_out, mla_w_in, mla_q_norm_g, mla_kv_norm_g, mla_w_uq, mla_w_ukv,
           mla_q_head_g, mla_k_head_g, mla_w_out):
    b, s, d = x.shape
    depth = ada_w.shape[0]
    c_pad = jnp.pad(c.astype(F32), ((0, 8 - b), (0, 0)))
    mod = _ada(c_pad, ada_w, ada_b)[:, :b].reshape(depth, b, 6, d)
    cos2, sin2 = _rope_tables(s)

    for l in range(depth):
        g1 = norm1_g[l].reshape(1, d)
        g2 = norm2_g[l].reshape(1, d)
        if l % 2 == 0:
            e = l // 2
            mix = _even_mixer_pre(
                x, mod[l], g1, _group_in_proj(ab_w_in[e]).astype(BF16), sgu_norm_g[e].reshape(1, D_A), sgu_w[e],
                sgu_b[e].T, conv_w[e], conv_b[e].reshape(1, D_B), conv_ln_g[e].reshape(1, D_B),
                conv_ln_b[e].reshape(1, D_B))
            w_out = ab_w_out[e]
        else:
            o = l // 2
            w_in2, w_uq2, w_uk, w_uv, qhg2, khg2 = _prep_mla(
                mla_w_in[o], mla_w_uq[o], mla_w_ukv[o], mla_q_head_g[o], mla_k_head_g[o])
            qn, qr, kn, kr2, vt = _mla_pre(
                x, mod[l], g1, w_in2, mla_q_norm_g[o].reshape(1, Q_RANK), mla_kv_norm_g[o].reshape(1, KV_RANK),
                w_uq2, w_uk, w_uv, qhg2, khg2, cos2, sin2)
            mix = _attention(qn, qr, kn, kr2, vt)
            w_out = mla_w_out[o]
        xn, h2 = _oproj(x, mix, mod[l], g2, w_out.astype(BF16))
        x = _mlp(h2, xn, mod[l], mlp_w1[l].astype(BF16), mlp_w2[l].astype(BF16))
    return x
```

```python
import jax
import jax.numpy as jnp
from jax import lax
from jax.experimental import pallas as pl
from jax.experimental.pallas import tpu as pltpu

F32 = jnp.float32
BF16 = jnp.bfloat16

D_MODEL = 2048
D_FF = 4 * D_MODEL
EPS = 1e-6
GROUP_DIM = 128
N_GROUPS = 8
CHUNK = 128
CONV_WIDTH = 31
D_A = D_MODEL // 2
D_B = D_MODEL - D_A
N_HEADS = 16
Q_RANK = 512
KV_RANK = 512
NOPE = 128
ROPE = 64
V_DIM = 128
QK_DIM = NOPE + ROPE
ROPE_THETA = 10000.0
ATTN_SCALE = QK_DIM ** -0.5

LANES = 128
SUBLANES = 8
ROWS = 128
HALO = 32
VMEM_CAP = 60 * 1024 * 1024

TM = 512
TM_MLA = 256
TF = 1024
TQ = 512


def _params(semantics, vmem_bytes):
    return pltpu.CompilerParams(dimension_semantics=semantics,
                                vmem_limit_bytes=min(int(vmem_bytes), VMEM_CAP))


def _resident(shape):
    nd = len(shape)
    return pl.BlockSpec(shape, lambda *_: (0,) * nd, pipeline_mode=pl.Buffered(1))


def _sigmoid(x):
    return 1.0 / (1.0 + jnp.exp(-x))


def _gelu_tanh(x):
    return 0.5 * x * (1.0 + jnp.tanh(0.7978845608028654 * (x + 0.044715 * (x * x * x))))


def _norm_mod(xb, g, shift, scale1p):
    ms = jnp.mean(xb * xb, axis=-1, keepdims=True)
    y = xb * lax.rsqrt(ms + EPS)
    return (y * g) * scale1p + shift


def _ada_kernel(c_ref, w_ref, b_ref, o_ref):
    c = c_ref[...]
    ca = (c * _sigmoid(c)).astype(BF16)
    o_ref[0] = jnp.dot(ca, w_ref[0].astype(BF16), preferred_element_type=F32) + b_ref[0]


def _ada(c_pad, ada_w, ada_b):
    depth, d, n = ada_w.shape
    tn = 1024
    rows = c_pad.shape[0]
    return pl.pallas_call(
        _ada_kernel,
        grid=(depth, n // tn),
        in_specs=[
            pl.BlockSpec((rows, d), lambda l, j: (0, 0)),
            pl.BlockSpec((1, d, tn), lambda l, j: (l, 0, j)),
            pl.BlockSpec((1, 1, tn), lambda l, j: (l, 0, j)),
        ],
        out_specs=pl.BlockSpec((1, rows, tn), lambda l, j: (l, 0, j)),
        out_shape=jax.ShapeDtypeStruct((depth, rows, n), F32),
        compiler_params=_params(("arbitrary", "arbitrary"), 2 * d * tn * 4 + 8 * d * tn + (4 << 20)),
        name="ada_mod",
    )(c_pad, ada_w, ada_b.reshape(depth, 1, n))


def _even_kernel(x_ref, mod_ref, g1_ref, win_ref, sng_ref, sw_ref, sbt_ref, cw_ref, cb_ref, lg_ref, lb_ref,
                 out_ref, h_ref, proj_ref, ybuf_ref, conv_ref):
    tm = x_ref.shape[0]
    nchunk = tm // ROWS
    shift = mod_ref[0:1, :]
    scale1p = 1.0 + mod_ref[1:2, :]
    g1 = g1_ref[...]

    def norm_body(c, carry):
        r0 = pl.multiple_of(c * ROWS, ROWS)
        h_ref[pl.ds(r0, ROWS), :] = _norm_mod(x_ref[pl.ds(r0, ROWS), :], g1, shift, scale1p).astype(BF16)
        return carry

    lax.fori_loop(0, nchunk, norm_body, 0)

    @pl.when(pl.program_id(1) == 0)
    def _():
        ybuf_ref[:, 0:HALO, :] = jnp.zeros((N_GROUPS, HALO, LANES), F32)

    row = lax.broadcasted_iota(jnp.int32, (CHUNK, CHUNK), 0)
    col = lax.broadcasted_iota(jnp.int32, (CHUNK, CHUNK), 1)
    causal = col <= row
    base = HALO - (CONV_WIDTH - 1)

    for j in range(N_GROUPS):
        lo, hi = j * LANES, (j + 1) * LANES
        proj_ref[j] = jnp.dot(h_ref[...], win_ref[j], preferred_element_type=F32)
        w = jnp.where(causal, sw_ref[j], 0.0).astype(BF16)
        for c in range(nchunk):
            r0 = c * ROWS
            u = _gelu_tanh(proj_ref[j, r0:r0 + ROWS, 0:LANES])
            v = _gelu_tanh(proj_ref[j, r0:r0 + ROWS, LANES:2 * LANES])
            vn = v * lax.rsqrt(jnp.mean(v * v, axis=-1, keepdims=True) + EPS) * sng_ref[0:1, lo:hi]
            mixed = jnp.dot(w, vn.astype(BF16), preferred_element_type=F32) + sbt_ref[:, j:j + 1]
            out_ref[r0:r0 + ROWS, lo:hi] = (u * mixed).astype(BF16)
            a = proj_ref[j, r0:r0 + ROWS, 2 * LANES:3 * LANES]
            gate = proj_ref[j, r0:r0 + ROWS, 3 * LANES:4 * LANES]
            ybuf_ref[j, HALO + r0:HALO + r0 + ROWS, :] = a * _sigmoid(gate)
        for c in range(nchunk):
            acc = jnp.broadcast_to(cb_ref[0:1, lo:hi], (ROWS, LANES))
            for r in range(SUBLANES):
                ext = SUBLANES if r else 0
                part = None
                for k in range(CONV_WIDTH):
                    if (base + k) % SUBLANES != r:
                        continue
                    o = c * ROWS + SUBLANES * ((base + k) // SUBLANES)
                    term = ybuf_ref[j, o:o + ROWS + ext, :] * cw_ref[k:k + 1, lo:hi]
                    part = term if part is None else part + term
                acc = acc + part[r:r + ROWS]
            conv_ref[c * ROWS:(c + 1) * ROWS, lo:hi] = acc
        ybuf_ref[j, 0:HALO, :] = ybuf_ref[j, tm:tm + HALO, :]

    lg = lg_ref[...]
    lb = lb_ref[...]

    def ln_body(c, carry):
        r0 = pl.multiple_of(c * ROWS, ROWS)
        y = conv_ref[pl.ds(r0, ROWS), :]
        mu = jnp.mean(y, axis=-1, keepdims=True)
        d = y - mu
        var = jnp.mean(d * d, axis=-1, keepdims=True)
        z = (d * lax.rsqrt(var + EPS)) * lg + lb
        out_ref[pl.ds(r0, ROWS), D_A:D_A + D_B] = (z * _sigmoid(z)).astype(BF16)
        return carry

    lax.fori_loop(0, nchunk, ln_body, 0)


def _even_mixer_pre(x, mod, g1, w_in, sng, sgu_w, sgu_bt, conv_w, conv_b, ln_g, ln_b):
    b, s, d = x.shape
    n_in = w_in.shape[0] * w_in.shape[2]
    vmem = (2 * TM * d * 4 + d * n_in * 2 + 2 * TM * d * 2 + TM * n_in * 4 + TM * d * 2
            + (TM + HALO) * D_B * 4 + TM * D_B * 4 + (6 << 20))
    return pl.pallas_call(
        _even_kernel,
        grid=(b, s // TM),
        in_specs=[
            pl.BlockSpec((None, TM, d), lambda i, m: (i, m, 0)),
            pl.BlockSpec((None, 6, d), lambda i, m: (i, 0, 0)),
            _resident((1, d)),
            _resident(w_in.shape),
            _resident((1, D_A)),
            _resident((N_GROUPS, CHUNK, CHUNK)),
            _resident((CHUNK, N_GROUPS)),
            _resident((CONV_WIDTH, D_B)),
            _resident((1, D_B)),
            _resident((1, D_B)),
            _resident((1, D_B)),
        ],
        out_specs=pl.BlockSpec((None, TM, d), lambda i, m: (i, m, 0)),
        out_shape=jax.ShapeDtypeStruct((b, s, d), BF16),
        scratch_shapes=[
            pltpu.VMEM((TM, d), BF16),
            pltpu.VMEM((N_GROUPS, TM, 4 * LANES), F32),
            pltpu.VMEM((N_GROUPS, TM + HALO, LANES), F32),
            pltpu.VMEM((TM, D_B), F32),
        ],
        compiler_params=_params(("arbitrary", "arbitrary"), vmem),
        name="even_mixer",
    )(x, mod, g1, w_in, sng, sgu_w, sgu_bt, conv_w, conv_b, ln_g, ln_b)


def _group_in_proj(w_in):
    d = w_in.shape[0]
    parts = w_in.reshape(d, 4, N_GROUPS, LANES)
    return jnp.transpose(parts, (2, 0, 1, 3)).reshape(N_GROUPS, d, 4 * LANES)


def _mla_pre_kernel(x_ref, mod_ref, g1_ref, win_ref, qg_ref, kvg_ref, wuq_ref, wuk_ref, wuv_ref,
                    qhg_ref, khg_ref, cos_ref, sin_ref,
                    qn_ref, qr_ref, kn_ref, kr_ref, v_ref,
                    h_ref, proj_ref, cq_ref, ckv_ref, qbig_ref, kbig_ref):
    tm = x_ref.shape[0]
    nchunk = tm // ROWS
    shift = mod_ref[0:1, :]
    scale1p = 1.0 + mod_ref[1:2, :]
    g1 = g1_ref[...]

    def norm_body(c, carry):
        r0 = pl.multiple_of(c * ROWS, ROWS)
        h_ref[pl.ds(r0, ROWS), :] = _norm_mod(x_ref[pl.ds(r0, ROWS), :], g1, shift, scale1p).astype(BF16)
        return carry

    lax.fori_loop(0, nchunk, norm_body, 0)
    proj_ref[...] = jnp.dot(h_ref[...], win_ref[...], preferred_element_type=F32)

    qg = qg_ref[...]
    kvg = kvg_ref[...]
    lane = lax.broadcasted_iota(jnp.int32, (ROWS, LANES), 1)
    low = lane < ROPE
    kg_nope = khg_ref[0:1, :]
    kg_lo, kg_lo_sw = khg_ref[1:2, :], khg_ref[2:3, :]
    kg_hi, kg_hi_sw = khg_ref[3:4, :], khg_ref[4:5, :]

    for c in range(nchunk):
        r0 = c * ROWS
        cq = proj_ref[r0:r0 + ROWS, 0:Q_RANK]
        cq = cq * lax.rsqrt(jnp.mean(cq * cq, axis=-1, keepdims=True) + EPS) * qg
        cq_ref[r0:r0 + ROWS, :] = cq.astype(BF16)
        ckv = proj_ref[r0:r0 + ROWS, Q_RANK:Q_RANK + KV_RANK]
        ckv = ckv * lax.rsqrt(jnp.mean(ckv * ckv, axis=-1, keepdims=True) + EPS) * kvg
        ckv_ref[r0:r0 + ROWS, :] = ckv.astype(BF16)
        cos = cos_ref[r0:r0 + ROWS, :]
        sin = sin_ref[r0:r0 + ROWS, :]
        o = Q_RANK + KV_RANK
        for j, (ga, gb) in enumerate(((kg_lo, kg_lo_sw), (kg_hi, kg_hi_sw))):
            kr = proj_ref[r0:r0 + ROWS, o + 2 * j * LANES:o + (2 * j + 1) * LANES]
            krs = proj_ref[r0:r0 + ROWS, o + (2 * j + 1) * LANES:o + (2 * j + 2) * LANES]
            inv = lax.rsqrt(jnp.sum(kr * kr, axis=-1, keepdims=True) * (1.0 / ROPE) + EPS)
            kr_ref[j, r0:r0 + ROWS, :] = ((kr * inv * ga) * cos + (krs * inv * gb) * sin).astype(BF16)

    n_nope = N_HEADS * NOPE
    n_rope = N_HEADS * ROPE
    qbig_ref[...] = jnp.dot(cq_ref[...], wuq_ref[...], preferred_element_type=F32)
    kbig_ref[...] = jnp.dot(ckv_ref[...], wuk_ref[...], preferred_element_type=F32)
    vt = lax.dot_general(wuv_ref[...], ckv_ref[...], (((1,), (1,)), ((), ())), preferred_element_type=F32)
    for hd in range(N_HEADS):
        v_ref[hd] = vt[hd * V_DIM:(hd + 1) * V_DIM, :].astype(BF16)

    qg_nope = qhg_ref[0:1, :]
    qg_r, qg_r_sw = qhg_ref[1:2, :], qhg_ref[2:3, :]
    for c in range(nchunk):
        r0 = c * ROWS
        for hd in range(N_HEADS):
            q = qbig_ref[r0:r0 + ROWS, hd * NOPE:(hd + 1) * NOPE]
            inv = lax.rsqrt(jnp.mean(q * q, axis=-1, keepdims=True) + EPS)
            qn_ref[r0:r0 + ROWS, hd * NOPE:(hd + 1) * NOPE] = ((q * inv * qg_nope) * ATTN_SCALE).astype(BF16)
        cos = cos_ref[r0:r0 + ROWS, :]
        sin = sin_ref[r0:r0 + ROWS, :]
        for p in range(n_rope // LANES):
            q = qbig_ref[r0:r0 + ROWS, n_nope + p * LANES:n_nope + (p + 1) * LANES]
            qs = qbig_ref[r0:r0 + ROWS, n_nope + n_rope + p * LANES:n_nope + n_rope + (p + 1) * LANES]
            sq = q * q
            s_lo = jnp.sum(jnp.where(low, sq, 0.0), axis=-1, keepdims=True)
            s_hi = jnp.sum(jnp.where(low, 0.0, sq), axis=-1, keepdims=True)
            inv = lax.rsqrt(jnp.where(low, s_lo, s_hi) * (1.0 / ROPE) + EPS)
            r = (q * inv * qg_r) * cos + (qs * inv * qg_r_sw) * sin
            qr_ref[r0:r0 + ROWS, p * LANES:(p + 1) * LANES] = (r * ATTN_SCALE).astype(BF16)
        for hd in range(N_HEADS):
            k = kbig_ref[r0:r0 + ROWS, hd * NOPE:(hd + 1) * NOPE]
            inv = lax.rsqrt(jnp.mean(k * k, axis=-1, keepdims=True) + EPS)
            kn_ref[r0:r0 + ROWS, hd * NOPE:(hd + 1) * NOPE] = (k * inv * kg_nope).astype(BF16)


def _mla_pre(x, mod, g1, w_in, qg, kvg, w_uq, w_uk, w_uv, qhg, khg, cos2, sin2):
    b, s, d = x.shape
    n_in = w_in.shape[1]
    n_q = w_uq.shape[1]
    n_nope = N_HEADS * NOPE
    n_rope = N_HEADS * ROPE
    weights = (d * n_in + Q_RANK * n_q + 2 * KV_RANK * n_nope) * 2
    tm = TM_MLA
    vmem = (2 * tm * d * 4 + weights + 2 * tm * (3 * n_nope + n_rope + 2 * LANES) * 2 + 4 * tm * LANES * 4
            + tm * d * 2 + tm * n_in * 4 + 2 * tm * Q_RANK * 2 + tm * (n_q + n_nope) * 4 + (8 << 20))
    tile = lambda n: pl.BlockSpec((None, tm, n), lambda i, m: (i, m, 0))
    return pl.pallas_call(
        _mla_pre_kernel,
        grid=(b, s // tm),
        in_specs=[
            tile(d),
            pl.BlockSpec((None, 6, d), lambda i, m: (i, 0, 0)),
            _resident((1, d)),
            _resident((d, n_in)),
            _resident((1, Q_RANK)),
            _resident((1, KV_RANK)),
            _resident((Q_RANK, n_q)),
            _resident((KV_RANK, n_nope)),
            _resident((n_nope, KV_RANK)),
            _resident((8, LANES)),
            _resident((8, LANES)),
            pl.BlockSpec((tm, LANES), lambda i, m: (m, 0)),
            pl.BlockSpec((tm, LANES), lambda i, m: (m, 0)),
        ],
        out_specs=[
            tile(n_nope),
            tile(n_rope),
            tile(n_nope),
            pl.BlockSpec((None, 2, tm, LANES), lambda i, m: (i, 0, m, 0)),
            pl.BlockSpec((None, N_HEADS, None, V_DIM, tm), lambda i, m: (i, 0, m // (TQ // tm), 0, m % (TQ // tm))),
        ],
        out_shape=[
            jax.ShapeDtypeStruct((b, s, n_nope), BF16),
            jax.ShapeDtypeStruct((b, s, n_rope), BF16),
            jax.ShapeDtypeStruct((b, s, n_nope), BF16),
            jax.ShapeDtypeStruct((b, 2, s, LANES), BF16),
            jax.ShapeDtypeStruct((b, N_HEADS, s // TQ, V_DIM, TQ), BF16),
        ],
        scratch_shapes=[
            pltpu.VMEM((tm, d), BF16),
            pltpu.VMEM((tm, n_in), F32),
            pltpu.VMEM((tm, Q_RANK), BF16),
            pltpu.VMEM((tm, KV_RANK), BF16),
            pltpu.VMEM((tm, n_q), F32),
            pltpu.VMEM((tm, n_nope), F32),
        ],
        compiler_params=_params(("arbitrary", "arbitrary"), vmem),
        name="mla_proj",
    )(x, mod, g1, w_in, qg, kvg, w_uq, w_uk, w_uv, qhg, khg, cos2, sin2)


def _attn_kernel(qn_ref, qr_ref, kn_ref, kr_ref, vt_ref, o_ref):
    tq = qn_ref.shape[0]
    hd = pl.program_id(1)
    qi = pl.program_id(2)
    lane = lax.broadcasted_iota(jnp.int32, (tq, LANES), 1)
    lo = (hd % 2) * ROPE
    keep = jnp.logical_and(lane >= lo, lane < lo + ROPE)
    qr = jnp.where(keep, qr_ref[...], jnp.zeros((), BF16))
    q = jnp.concatenate([qn_ref[...], qr], axis=-1)

    def step(kb, nblk, carry, diagonal):
        m, l, acc = carry
        k0 = pl.multiple_of(kb * tq, tq)
        rows = nblk * tq
        k = jnp.concatenate([kn_ref[pl.ds(k0, rows), :], kr_ref[pl.ds(k0, rows), :]], axis=-1)
        s = lax.dot_general(k, q, (((1,), (1,)), ((), ())), preferred_element_type=F32)
        if diagonal:
            key = lax.broadcasted_iota(jnp.int32, (rows, tq), 0)
            qry = lax.broadcasted_iota(jnp.int32, (rows, tq), 1) + (rows - tq)
            s = jnp.where(key <= qry, s, -jnp.inf)
        m_new = jnp.maximum(m, jnp.max(s, axis=0, keepdims=True))
        alpha = jnp.exp(m - m_new)
        p = jnp.exp(s - m_new)
        l = alpha * l + jnp.sum(p, axis=0, keepdims=True)
        p = p.astype(BF16)
        acc = alpha * acc
        for i in range(nblk):
            acc = acc + jnp.dot(vt_ref[kb + i], p[i * tq:(i + 1) * tq], preferred_element_type=F32)
        return m_new, l, acc

    init = (jnp.full((1, tq), -jnp.inf, F32), jnp.zeros((1, tq), F32), jnp.zeros((V_DIM, tq), F32))
    carry = lax.fori_loop(0, qi // 2, lambda j, cr: step(2 * j, 2, cr, False), init)
    odd = qi % 2
    carry = lax.fori_loop(0, odd, lambda j, cr: step(qi - 1, 2, cr, True), carry)
    carry = lax.fori_loop(0, 1 - odd, lambda j, cr: step(qi, 1, cr, True), carry)
    _, l, acc = carry
    o_ref[...] = jnp.transpose(acc / l).astype(BF16)


def _attention(qn, qr, kn, kr2, vt):
    b, s, _ = qn.shape
    vmem = 6 * s * LANES * 2 + 8 * TQ * LANES * 2 + 12 * TQ * TQ * 4 + (8 << 20)
    return pl.pallas_call(
        _attn_kernel,
        grid=(b, N_HEADS, s // TQ),
        in_specs=[
            pl.BlockSpec((None, TQ, NOPE), lambda i, h, q: (i, q, h)),
            pl.BlockSpec((None, TQ, LANES), lambda i, h, q: (i, q, h // 2)),
            pl.BlockSpec((None, s, NOPE), lambda i, h, q: (i, 0, h)),
            pl.BlockSpec((None, None, s, LANES), lambda i, h, q: (i, h % 2, 0, 0)),
            pl.BlockSpec((None, None, s // TQ, V_DIM, TQ), lambda i, h, q: (i, h, 0, 0, 0)),
        ],
        out_specs=pl.BlockSpec((None, TQ, V_DIM), lambda i, h, q: (i, q, h)),
        out_shape=jax.ShapeDtypeStruct((b, s, N_HEADS * V_DIM), BF16),
        compiler_params=_params(("arbitrary", "arbitrary", "arbitrary"), vmem),
        name="mla_attention",
    )(qn, qr, kn, kr2, vt)


def _oproj_kernel(x_ref, mix_ref, mod_ref, g2_ref, wo_ref, xn_ref, h_ref):
    tm = x_ref.shape[0]
    gate1 = mod_ref[2:3, :]
    shift2 = mod_ref[3:4, :]
    scale2p = 1.0 + mod_ref[4:5, :]
    g2 = g2_ref[...]
    xn_ref[...] = jnp.dot(mix_ref[...], wo_ref[...], preferred_element_type=F32)

    def body(c, carry):
        r0 = pl.multiple_of(c * ROWS, ROWS)
        xn = x_ref[pl.ds(r0, ROWS), :] + gate1 * xn_ref[pl.ds(r0, ROWS), :]
        xn_ref[pl.ds(r0, ROWS), :] = xn
        h_ref[pl.ds(r0, ROWS), :] = _norm_mod(xn, g2, shift2, scale2p).astype(BF16)
        return carry

    lax.fori_loop(0, tm // ROWS, body, 0)


def _oproj(x, mix, mod, g2, w_out):
    b, s, d = x.shape
    vmem = 2 * TM * d * 4 + 2 * TM * d * 2 + d * d * 2 + 2 * TM * d * 4 + 2 * TM * d * 2 + (6 << 20)
    tile = pl.BlockSpec((None, TM, d), lambda i, m: (i, m, 0))
    return pl.pallas_call(
        _oproj_kernel,
        grid=(b, s // TM),
        in_specs=[tile, tile, pl.BlockSpec((None, 6, d), lambda i, m: (i, 0, 0)), _resident((1, d)),
                  _resident((d, d))],
        out_specs=[tile, tile],
        out_shape=[jax.ShapeDtypeStruct((b, s, d), F32), jax.ShapeDtypeStruct((b, s, d), BF16)],
        compiler_params=_params(("arbitrary", "arbitrary"), vmem),
        name="out_proj",
    )(x, mix, mod, g2, w_out)


def _mlp_kernel(h_ref, x_ref, mod_ref, w1_ref, w2_ref, out_ref, acc_ref):
    j = pl.program_id(2)

    @pl.when(j == 0)
    def _():
        acc_ref[...] = jnp.zeros(acc_ref.shape, F32)

    a = jnp.dot(h_ref[...], w1_ref[...], preferred_element_type=F32)
    r = jnp.square(jnp.maximum(a, 0.0)).astype(BF16)
    acc_ref[...] += jnp.dot(r, w2_ref[...], preferred_element_type=F32)

    @pl.when(j == pl.num_programs(2) - 1)
    def _():
        out_ref[...] = x_ref[...] + mod_ref[5:6, :] * acc_ref[...]


def _mlp(h, x, mod, w1, w2):
    b, s, d = x.shape
    dff = w1.shape[1]
    vmem = (2 * TM * d * 2 + 2 * TM * d * 4 + 4 * d * TF * 2 + 2 * TM * d * 4 + TM * d * 4
            + TM * TF * 6 + TM * d * 4 + (4 << 20))
    tile = pl.BlockSpec((None, TM, d), lambda i, m, j: (i, m, 0))
    return pl.pallas_call(
        _mlp_kernel,
        grid=(b, s // TM, dff // TF),
        in_specs=[
            tile,
            tile,
            pl.BlockSpec((None, 6, d), lambda i, m, j: (i, 0, 0)),
            pl.BlockSpec((d, TF), lambda i, m, j: (0, j)),
            pl.BlockSpec((TF, d), lambda i, m, j: (j, 0)),
        ],
        out_specs=tile,
        out_shape=jax.ShapeDtypeStruct((b, s, d), F32),
        scratch_shapes=[pltpu.VMEM((TM, d), F32)],
        compiler_params=_params(("arbitrary", "arbitrary", "arbitrary"), vmem),
        name="relu2_mlp",
    )(h, x, mod, w1, w2)


def _rope_tables(s):
    pos = jnp.arange(s, dtype=F32)
    inv = ROPE_THETA ** (-jnp.arange(0, ROPE, 2, dtype=F32) / ROPE)
    ang = pos[:, None] * inv[None, :]
    cos, sin = jnp.cos(ang), jnp.sin(ang)
    return jnp.tile(cos, (1, 4)), jnp.tile(sin, (1, 4))


def _swap_halves(w):
    half = ROPE // 2
    return jnp.concatenate([-w[..., half:], w[..., :half]], axis=-1)


def _swap_gain(g):
    half = ROPE // 2
    return jnp.concatenate([g[half:], g[:half]])


def _rows8(rows):
    rows = [jnp.asarray(r, F32) for r in rows]
    rows += [jnp.zeros((LANES,), F32)] * (8 - len(rows))
    return jnp.stack(rows)


def _prep_mla(w_in, w_uq, w_ukv, qhg, khg):
    d = w_in.shape[0]
    zeros = jnp.zeros((d, ROPE), F32)
    w_kr = w_in[:, Q_RANK + KV_RANK:]
    w_krs = _swap_halves(w_kr)
    w_in2 = jnp.concatenate([
        w_in[:, :Q_RANK + KV_RANK],
        w_kr, zeros, w_krs, zeros,
        zeros, w_kr, zeros, w_krs,
    ], axis=1).astype(BF16)
    uq = w_uq.reshape(Q_RANK, N_HEADS, QK_DIM)
    uq_rope = uq[:, :, NOPE:]
    w_uq2 = jnp.concatenate([
        uq[:, :, :NOPE].reshape(Q_RANK, N_HEADS * NOPE),
        uq_rope.reshape(Q_RANK, N_HEADS * ROPE),
        _swap_halves(uq_rope).reshape(Q_RANK, N_HEADS * ROPE),
    ], axis=1).astype(BF16)
    ukv = w_ukv.reshape(KV_RANK, N_HEADS, NOPE + V_DIM)
    w_uk = ukv[:, :, :NOPE].reshape(KV_RANK, N_HEADS * NOPE).astype(BF16)
    w_uv = ukv[:, :, NOPE:].reshape(KV_RANK, N_HEADS * V_DIM).T.astype(BF16)
    z = jnp.zeros((ROPE,), F32)
    qr, kr = qhg[NOPE:], khg[NOPE:]
    qhg2 = _rows8([qhg[:NOPE], jnp.tile(qr, 2), jnp.tile(_swap_gain(qr), 2)])
    khg2 = _rows8([khg[:NOPE],
                   jnp.concatenate([kr, z]), jnp.concatenate([_swap_gain(kr), z]),
                   jnp.concatenate([z, kr]), jnp.concatenate([z, _swap_gain(kr)])])
    return w_in2, w_uq2, w_uk, w_uv, qhg2, khg2


def kernel(x, c, norm1_g, norm2_g, ada_w, ada_b, mlp_w1, mlp_w2, ab_w_in, sgu_norm_g, sgu_w, sgu_b, conv_w, conv_b,
           conv_ln_g, conv_ln_b, ab_w_out, mla_w_in, mla_q_norm_g, mla_kv_norm_g, mla_w_uq, mla_w_ukv,
           mla_q_head_g, mla_k_head_g, mla_w_out):
    b, s, d = x.shape
    depth = ada_w.shape[0]
    c_pad = jnp.pad(c.astype(F32), ((0, 8 - b), (0, 0)))
    mod = _ada(c_pad, ada_w, ada_b)[:, :b].reshape(depth, b, 6, d)
    cos2, sin2 = _rope_tables(s)

    for l in range(depth):
        g1 = norm1_g[l].reshape(1, d)
        g2 = norm2_g[l].reshape(1, d)
        if l % 2 == 0:
            e = l // 2
            mix = _even_mixer_pre(
                x, mod[l], g1, _group_in_proj(ab_w_in[e]).astype(BF16), sgu_norm_g[e].reshape(1, D_A), sgu_w[e],
                sgu_b[e].T, conv_w[e], conv_b[e].reshape(1, D_B), conv_ln_g[e].reshape(1, D_B),
                conv_ln_b[e].reshape(1, D_B))
            w_out = ab_w_out[e]
        else:
            o = l // 2
            w_in2, w_uq2, w_uk, w_uv, qhg2, khg2 = _prep_mla(
                mla_w_in[o], mla_w_uq[o], mla_w_ukv[o], mla_q_head_g[o], mla_k_head_g[o])
            qn, qr, kn, kr2, vt = _mla_pre(
                x, mod[l], g1, w_in2, mla_q_norm_g[o].reshape(1, Q_RANK), mla_kv_norm_g[o].reshape(1, KV_RANK),
                w_uq2, w_uk, w_uv, qhg2, khg2, cos2, sin2)
            mix = _attention(qn, qr, kn, kr2, vt)
            w_out = mla_w_out[o]
        xn, h2 = _oproj(x, mix, mod[l], g2, w_out.astype(BF16))
        x = _mlp(h2, xn, mod[l], mlp_w1[l].astype(BF16), mlp_w2[l].astype(BF16))
    return x
```

```python
import jax
import jax.numpy as jnp
from jax import lax
from jax.experimental import pallas as pl
from jax.experimental.pallas import tpu as pltpu

F32 = jnp.float32
BF16 = jnp.bfloat16

D_MODEL = 2048
D_FF = 4 * D_MODEL
EPS = 1e-6
GROUP_DIM = 128
N_GROUPS = 8
CHUNK = 128
CONV_WIDTH = 31
D_A = D_MODEL // 2
D_B = D_MODEL - D_A
N_HEADS = 16
Q_RANK = 512
KV_RANK = 512
NOPE = 128
ROPE = 64
V_DIM = 128
QK_DIM = NOPE + ROPE
ROPE_THETA = 10000.0
ATTN_SCALE = QK_DIM ** -0.5

LANES = 128
SUBLANES = 8
ROWS = 128
HALO = 32
VMEM_CAP = 60 * 1024 * 1024

TM = 512
TM_MLA = 256
TF = 1024
TQ = 512


def _params(semantics, vmem_bytes):
    return pltpu.CompilerParams(dimension_semantics=semantics,
                                vmem_limit_bytes=min(int(vmem_bytes), VMEM_CAP))


def _resident(shape):
    nd = len(shape)
    return pl.BlockSpec(shape, lambda *_: (0,) * nd, pipeline_mode=pl.Buffered(1))


def _sigmoid(x):
    return 1.0 / (1.0 + jnp.exp(-x))


def _gelu_tanh(x):
    return 0.5 * x * (1.0 + jnp.tanh(0.7978845608028654 * (x + 0.044715 * (x * x * x))))


def _norm_mod(xb, g, shift, scale1p):
    ms = jnp.mean(xb * xb, axis=-1, keepdims=True)
    y = xb * lax.rsqrt(ms + EPS)
    return (y * g) * scale1p + shift


def _ada_kernel(c_ref, w_ref, b_ref, o_ref):
    c = c_ref[...]
    ca = (c * _sigmoid(c)).astype(BF16)
    o_ref[0] = jnp.dot(ca, w_ref[0].astype(BF16), preferred_element_type=F32) + b_ref[0]


def _ada(c_pad, ada_w, ada_b):
    depth, d, n = ada_w.shape
    tn = 1024
    rows = c_pad.shape[0]
    return pl.pallas_call(
        _ada_kernel,
        grid=(depth, n // tn),
        in_specs=[
            pl.BlockSpec((rows, d), lambda l, j: (0, 0)),
            pl.BlockSpec((1, d, tn), lambda l, j: (l, 0, j)),
            pl.BlockSpec((1, 1, tn), lambda l, j: (l, 0, j)),
        ],
        out_specs=pl.BlockSpec((1, rows, tn), lambda l, j: (l, 0, j)),
        out_shape=jax.ShapeDtypeStruct((depth, rows, n), F32),
        compiler_params=_params(("arbitrary", "arbitrary"), 2 * d * tn * 4 + 8 * d * tn + (4 << 20)),
        name="ada_mod",
    )(c_pad, ada_w, ada_b.reshape(depth, 1, n))


def _even_kernel(x_ref, mod_ref, g1_ref, win_ref, sng_ref, sw_ref, sbt_ref, cw_ref, cb_ref, lg_ref, lb_ref,
                 out_ref, h_ref, proj_ref, ybuf_ref, conv_ref):
    tm = x_ref.shape[0]
    nchunk = tm // ROWS
    shift = mod_ref[0:1, :]
    scale1p = 1.0 + mod_ref[1:2, :]
    g1 = g1_ref[...]

    def norm_body(c, carry):
        r0 = pl.multiple_of(c * ROWS, ROWS)
        h_ref[pl.ds(r0, ROWS), :] = _norm_mod(x_ref[pl.ds(r0, ROWS), :], g1, shift, scale1p).astype(BF16)
        return carry

    lax.fori_loop(0, nchunk, norm_body, 0)

    @pl.when(pl.program_id(1) == 0)
    def _():
        ybuf_ref[:, 0:HALO, :] = jnp.zeros((N_GROUPS, HALO, LANES), F32)

    row = lax.broadcasted_iota(jnp.int32, (CHUNK, CHUNK), 0)
    col = lax.broadcasted_iota(jnp.int32, (CHUNK, CHUNK), 1)
    causal = col <= row
    base = HALO - (CONV_WIDTH - 1)

    for j in range(N_GROUPS):
        lo, hi = j * LANES, (j + 1) * LANES
        proj_ref[j] = jnp.dot(h_ref[...], win_ref[j], preferred_element_type=F32)
        w = jnp.where(causal, sw_ref[j], 0.0).astype(BF16)
        for c in range(nchunk):
            r0 = c * ROWS
            u = _gelu_tanh(proj_ref[j, r0:r0 + ROWS, 0:LANES])
            v = _gelu_tanh(proj_ref[j, r0:r0 + ROWS, LANES:2 * LANES])
            vn = v * lax.rsqrt(jnp.mean(v * v, axis=-1, keepdims=True) + EPS) * sng_ref[0:1, lo:hi]
            mixed = jnp.dot(w, vn.astype(BF16), preferred_element_type=F32) + sbt_ref[:, j:j + 1]
            out_ref[r0:r0 + ROWS, lo:hi] = (u * mixed).astype(BF16)
            a = proj_ref[j, r0:r0 + ROWS, 2 * LANES:3 * LANES]
            gate = proj_ref[j, r0:r0 + ROWS, 3 * LANES:4 * LANES]
            ybuf_ref[j, HALO + r0:HALO + r0 + ROWS, :] = a * _sigmoid(gate)
        for c in range(nchunk):
            acc = jnp.broadcast_to(cb_ref[0:1, lo:hi], (ROWS, LANES))
            for r in range(SUBLANES):
                ext = SUBLANES if r else 0
                part = None
                for k in range(CONV_WIDTH):
                    if (base + k) % SUBLANES != r:
                        continue
                    o = c * ROWS + SUBLANES * ((base + k) // SUBLANES)
                    term = ybuf_ref[j, o:o + ROWS + ext, :] * cw_ref[k:k + 1, lo:hi]
                    part = term if part is None else part + term
                acc = acc + part[r:r + ROWS]
            conv_ref[c * ROWS:(c + 1) * ROWS, lo:hi] = acc
        ybuf_ref[j, 0:HALO, :] = ybuf_ref[j, tm:tm + HALO, :]

    lg = lg_ref[...]
    lb = lb_ref[...]

    def ln_body(c, carry):
        r0 = pl.multiple_of(c * ROWS, ROWS)
        y = conv_ref[pl.ds(r0, ROWS), :]
        mu = jnp.mean(y, axis=-1, keepdims=True)
        d = y - mu
        var = jnp.mean(d * d, axis=-1, keepdims=True)
        z = (d * lax.rsqrt(var + EPS)) * lg + lb
        out_ref[pl.ds(r0, ROWS), D_A:D_A + D_B] = (z * _sigmoid(z)).astype(BF16)
        return carry

    lax.fori_loop(0, nchunk, ln_body, 0)


def _even_mixer_pre(x, mod, g1, w_in, sng, sgu_w, sgu_bt, conv_w, conv_b, ln_g, ln_b):
    b, s, d = x.shape
    n_in = w_in.shape[0] * w_in.shape[2]
    vmem = (2 * TM * d * 4 + d * n_in * 2 + 2 * TM * d * 2 + TM * n_in * 4 + TM * d * 2
            + (TM + HALO) * D_B * 4 + TM * D_B * 4 + (6 << 20))
    return pl.pallas_call(
        _even_kernel,
        grid=(b, s // TM),
        in_specs=[
            pl.BlockSpec((None, TM, d), lambda i, m: (i, m, 0)),
            pl.BlockSpec((None, 6, d), lambda i, m: (i, 0, 0)),
            _resident((1, d)),
            _resident(w_in.shape),
            _resident((1, D_A)),
            _resident((N_GROUPS, CHUNK, CHUNK)),
            _resident((CHUNK, N_GROUPS)),
            _resident((CONV_WIDTH, D_B)),
            _resident((1, D_B)),
            _resident((1, D_B)),
            _resident((1, D_B)),
        ],
        out_specs=pl.BlockSpec((None, TM, d), lambda i, m: (i, m, 0)),
        out_shape=jax.ShapeDtypeStruct((b, s, d), BF16),
        scratch_shapes=[
            pltpu.VMEM((TM, d), BF16),
            pltpu.VMEM((N_GROUPS, TM, 4 * LANES), F32),
            pltpu.VMEM((N_GROUPS, TM + HALO, LANES), F32),
            pltpu.VMEM((TM, D_B), F32),
        ],
        compiler_params=_params(("arbitrary", "arbitrary"), vmem),
        name="even_mixer",
    )(x, mod, g1, w_in, sng, sgu_w, sgu_bt, conv_w, conv_b, ln_g, ln_b)


def _group_in_proj(w_in):
    d = w_in.shape[0]
    parts = w_in.reshape(d, 4, N_GROUPS, LANES)
    return jnp.transpose(parts, (2, 0, 1, 3)).reshape(N_GROUPS, d, 4 * LANES)


def _mla_pre_kernel(x_ref, mod_ref, g1_ref, win_ref, qg_ref, kvg_ref, wuq_ref, wuk_ref, wuv_ref,
                    qhg_ref, khg_ref, cos_ref, sin_ref,
                    qn_ref, qr_ref, kn_ref, kr_ref, v_ref,
                    h_ref, proj_ref, cq_ref, ckv_ref, qbig_ref, kbig_ref):
    tm = x_ref.shape[0]
    nchunk = tm // ROWS
    shift = mod_ref[0:1, :]
    scale1p = 1.0 + mod_ref[1:2, :]
    g1 = g1_ref[...]

    def norm_body(c, carry):
        r0 = pl.multiple_of(c * ROWS, ROWS)
        h_ref[pl.ds(r0, ROWS), :] = _norm_mod(x_ref[pl.ds(r0, ROWS), :], g1, shift, scale1p).astype(BF16)
        return carry

    lax.fori_loop(0, nchunk, norm_body, 0)
    proj_ref[...] = jnp.dot(h_ref[...], win_ref[...], preferred_element_type=F32)

    qg = qg_ref[...]
    kvg = kvg_ref[...]
    lane = lax.broadcasted_iota(jnp.int32, (ROWS, LANES), 1)
    low = lane < ROPE
    kg_nope = khg_ref[0:1, :]
    kg_lo, kg_lo_sw = khg_ref[1:2, :], khg_ref[2:3, :]
    kg_hi, kg_hi_sw = khg_ref[3:4, :], khg_ref[4:5, :]

    for c in range(nchunk):
        r0 = c * ROWS
        cq = proj_ref[r0:r0 + ROWS, 0:Q_RANK]
        cq = cq * lax.rsqrt(jnp.mean(cq * cq, axis=-1, keepdims=True) + EPS) * qg
        cq_ref[r0:r0 + ROWS, :] = cq.astype(BF16)
        ckv = proj_ref[r0:r0 + ROWS, Q_RANK:Q_RANK + KV_RANK]
        ckv = ckv * lax.rsqrt(jnp.mean(ckv * ckv, axis=-1, keepdims=True) + EPS) * kvg
        ckv_ref[r0:r0 + ROWS, :] = ckv.astype(BF16)
        cos = cos_ref[r0:r0 + ROWS, :]
        sin = sin_ref[r0:r0 + ROWS, :]
        o = Q_RANK + KV_RANK
        for j, (ga, gb) in enumerate(((kg_lo, kg_lo_sw), (kg_hi, kg_hi_sw))):
            kr = proj_ref[r0:r0 + ROWS, o + 2 * j * LANES:o + (2 * j + 1) * LANES]
            krs = proj_ref[r0:r0 + ROWS, o + (2 * j + 1) * LANES:o + (2 * j + 2) * LANES]
            inv = lax.rsqrt(jnp.sum(kr * kr, axis=-1, keepdims=True) * (1.0 / ROPE) + EPS)
            kr_ref[j, r0:r0 + ROWS, :] = ((kr * inv * ga) * cos + (krs * inv * gb) * sin).astype(BF16)

    n_nope = N_HEADS * NOPE
    n_rope = N_HEADS * ROPE
    qbig_ref[...] = jnp.dot(cq_ref[...], wuq_ref[...], preferred_element_type=F32)
    kbig_ref[...] = jnp.dot(ckv_ref[...], wuk_ref[...], preferred_element_type=F32)
    vt = lax.dot_general(wuv_ref[...], ckv_ref[...], (((1,), (1,)), ((), ())), preferred_element_type=F32)
    for hd in range(N_HEADS):
        v_ref[hd] = vt[hd * V_DIM:(hd + 1) * V_DIM, :].astype(BF16)

    qg_nope = qhg_ref[0:1, :]
    qg_r, qg_r_sw = qhg_ref[1:2, :], qhg_ref[2:3, :]
    for c in range(nchunk):
        r0 = c * ROWS
        for hd in range(N_HEADS):
            q = qbig_ref[r0:r0 + ROWS, hd * NOPE:(hd + 1) * NOPE]
            inv = lax.rsqrt(jnp.mean(q * q, axis=-1, keepdims=True) + EPS)
            qn_ref[r0:r0 + ROWS, hd * NOPE:(hd + 1) * NOPE] = ((q * inv * qg_nope) * ATTN_SCALE).astype(BF16)
        cos = cos_ref[r0:r0 + ROWS, :]
        sin = sin_ref[r0:r0 + ROWS, :]
        for p in range(n_rope // LANES):
            q = qbig_ref[r0:r0 + ROWS, n_nope + p * LANES:n_nope + (p + 1) * LANES]
            qs = qbig_ref[r0:r0 + ROWS, n_nope + n_rope + p * LANES:n_nope + n_rope + (p + 1) * LANES]
            sq = q * q
            s_lo = jnp.sum(jnp.where(low, sq, 0.0), axis=-1, keepdims=True)
            s_hi = jnp.sum(jnp.where(low, 0.0, sq), axis=-1, keepdims=True)
            inv = lax.rsqrt(jnp.where(low, s_lo, s_hi) * (1.0 / ROPE) + EPS)
            r = (q * inv * qg_r) * cos + (qs * inv * qg_r_sw) * sin
            qr_ref[r0:r0 + ROWS, p * LANES:(p + 1) * LANES] = (r * ATTN_SCALE).astype(BF16)
        for hd in range(N_HEADS):
            k = kbig_ref[r0:r0 + ROWS, hd * NOPE:(hd + 1) * NOPE]
            inv = lax.rsqrt(jnp.mean(k * k, axis=-1, keepdims=True) + EPS)
            kn_ref[r0:r0 + ROWS, hd * NOPE:(hd + 1) * NOPE] = (k * inv * kg_nope).astype(BF16)


def _mla_pre(x, mod, g1, w_in, qg, kvg, w_uq, w_uk, w_uv, qhg, khg, cos2, sin2):
    b, s, d = x.shape
    n_in = w_in.shape[1]
    n_q = w_uq.shape[1]
    n_nope = N_HEADS * NOPE
    n_rope = N_HEADS * ROPE
    weights = (d * n_in + Q_RANK * n_q + 2 * KV_RANK * n_nope) * 2
    tm = TM_MLA
    vmem = (2 * tm * d * 4 + weights + 2 * tm * (3 * n_nope + n_rope + 2 * LANES) * 2 + 4 * tm * LANES * 4
            + tm * d * 2 + tm * n_in * 4 + 2 * tm * Q_RANK * 2 + tm * (n_q + n_nope) * 4 + (8 << 20))
    tile = lambda n: pl.BlockSpec((None, tm, n), lambda i, m: (i, m, 0))
    return pl.pallas_call(
        _mla_pre_kernel,
        grid=(b, s // tm),
        in_specs=[
            tile(d),
            pl.BlockSpec((None, 6, d), lambda i, m: (i, 0, 0)),
            _resident((1, d)),
            _resident((d, n_in)),
            _resident((1, Q_RANK)),
            _resident((1, KV_RANK)),
            _resident((Q_RANK, n_q)),
            _resident((KV_RANK, n_nope)),
            _resident((n_nope, KV_RANK)),
            _resident((8, LANES)),
            _resident((8, LANES)),
            pl.BlockSpec((tm, LANES), lambda i, m: (m, 0)),
            pl.BlockSpec((tm, LANES), lambda i, m: (m, 0)),
        ],
        out_specs=[
            tile(n_nope),
            tile(n_rope),
            tile(n_nope),
            pl.BlockSpec((None, 2, tm, LANES), lambda i, m: (i, 0, m, 0)),
            pl.BlockSpec((None, N_HEADS, None, V_DIM, tm), lambda i, m: (i, 0, m // (TQ // tm), 0, m % (TQ // tm))),
        ],
        out_shape=[
            jax.ShapeDtypeStruct((b, s, n_nope), BF16),
            jax.ShapeDtypeStruct((b, s, n_rope), BF16),
            jax.ShapeDtypeStruct((b, s, n_nope), BF16),
            jax.ShapeDtypeStruct((b, 2, s, LANES), BF16),
            jax.ShapeDtypeStruct((b, N_HEADS, s // TQ, V_DIM, TQ), BF16),
        ],
        scratch_shapes=[
            pltpu.VMEM((tm, d), BF16),
            pltpu.VMEM((tm, n_in), F32),
            pltpu.VMEM((tm, Q_RANK), BF16),
            pltpu.VMEM((tm, KV_RANK), BF16),
            pltpu.VMEM((tm, n_q), F32),
            pltpu.VMEM((tm, n_nope), F32),
        ],
        compiler_params=_params(("arbitrary", "arbitrary"), vmem),
        name="mla_proj",
    )(x, mod, g1, w_in, qg, kvg, w_uq, w_uk, w_uv, qhg, khg, cos2, sin2)


def _attn_kernel(qn_ref, qr_ref, kn_ref, kr_ref, vt_ref, o_ref):
    tq = qn_ref.shape[0]
    hd = pl.program_id(1)
    qi = pl.program_id(2)
    lane = lax.broadcasted_iota(jnp.int32, (tq, LANES), 1)
    lo = (hd % 2) * ROPE
    keep = jnp.logical_and(lane >= lo, lane < lo + ROPE)
    qr = jnp.where(keep, qr_ref[...], jnp.zeros((), BF16))
    q = jnp.concatenate([qn_ref[...], qr], axis=-1)

    def step(kb, nblk, carry, diagonal):
        m, l, acc = carry
        k0 = pl.multiple_of(kb * tq, tq)
        rows = nblk * tq
        k = jnp.concatenate([kn_ref[pl.ds(k0, rows), :], kr_ref[pl.ds(k0, rows), :]], axis=-1)
        s = lax.dot_general(k, q, (((1,), (1,)), ((), ())), preferred_element_type=F32)
        if diagonal:
            key = lax.broadcasted_iota(jnp.int32, (rows, tq), 0)
            qry = lax.broadcasted_iota(jnp.int32, (rows, tq), 1) + (rows - tq)
            s = jnp.where(key <= qry, s, -jnp.inf)
        m_new = jnp.maximum(m, jnp.max(s, axis=0, keepdims=True))
        alpha = jnp.exp(m - m_new)
        p = jnp.exp(s - m_new)
        l = alpha * l + jnp.sum(p, axis=0, keepdims=True)
        p = p.astype(BF16)
        acc = alpha * acc
        for i in range(nblk):
            acc = acc + jnp.dot(vt_ref[kb + i], p[i * tq:(i + 1) * tq], preferred_element_type=F32)
        return m_new, l, acc

    init = (jnp.full((1, tq), -jnp.inf, F32), jnp.zeros((1, tq), F32), jnp.zeros((V_DIM, tq), F32))
    carry = lax.fori_loop(0, qi // 2, lambda j, cr: step(2 * j, 2, cr, False), init)
    odd = qi % 2
    carry = lax.fori_loop(0, odd, lambda j, cr: step(qi - 1, 2, cr, True), carry)
    carry = lax.fori_loop(0, 1 - odd, lambda j, cr: step(qi, 1, cr, True), carry)
    _, l, acc = carry
    o_ref[...] = jnp.transpose(acc / l).astype(BF16)


def _attention(qn, qr, kn, kr2, vt):
    b, s, _ = qn.shape
    vmem = 6 * s * LANES * 2 + 8 * TQ * LANES * 2 + 12 * TQ * TQ * 4 + (8 << 20)
    return pl.pallas_call(
        _attn_kernel,
        grid=(b, N_HEADS, s // TQ),
        in_specs=[
            pl.BlockSpec((None, TQ, NOPE), lambda i, h, q: (i, q, h)),
            pl.BlockSpec((None, TQ, LANES), lambda i, h, q: (i, q, h // 2)),
            pl.BlockSpec((None, s, NOPE), lambda i, h, q: (i, 0, h)),
            pl.BlockSpec((None, None, s, LANES), lambda i, h, q: (i, h % 2, 0, 0)),
            pl.BlockSpec((None, None, s // TQ, V_DIM, TQ), lambda i, h, q: (i, h, 0, 0, 0)),
        ],
        out_specs=pl.BlockSpec((None, TQ, V_DIM), lambda i, h, q: (i, q, h)),
        out_shape=jax.ShapeDtypeStruct((b, s, N_HEADS * V_DIM), BF16),
        compiler_params=_params(("arbitrary", "arbitrary", "arbitrary"), vmem),
        name="mla_attention",
    )(qn, qr, kn, kr2, vt)


def _oproj_kernel(x_ref, mix_ref, mod_ref, g2_ref, wo_ref, xn_ref, h_ref):
    tm = x_ref.shape[0]
    gate1 = mod_ref[2:3, :]
    shift2 = mod_ref[3:4, :]
    scale2p = 1.0 + mod_ref[4:5, :]
    g2 = g2_ref[...]
    xn_ref[...] = jnp.dot(mix_ref[...], wo_ref[...], preferred_element_type=F32)

    def body(c, carry):
        r0 = pl.multiple_of(c * ROWS, ROWS)
        xn = x_ref[pl.ds(r0, ROWS), :] + gate1 * xn_ref[pl.ds(r0, ROWS), :]
        xn_ref[pl.ds(r0, ROWS), :] = xn
        h_ref[pl.ds(r0, ROWS), :] = _norm_mod(xn, g2, shift2, scale2p).astype(BF16)
        return carry

    lax.fori_loop(0, tm // ROWS, body, 0)


def _oproj(x, mix, mod, g2, w_out):
    b, s, d = x.shape
    vmem = 2 * TM * d * 4 + 2 * TM * d * 2 + d * d * 2 + 2 * TM * d * 4 + 2 * TM * d * 2 + (6 << 20)
    tile = pl.BlockSpec((None, TM, d), lambda i, m: (i, m, 0))
    return pl.pallas_call(
        _oproj_kernel,
        grid=(b, s // TM),
        in_specs=[tile, tile, pl.BlockSpec((None, 6, d), lambda i, m: (i, 0, 0)), _resident((1, d)),
                  _resident((d, d))],
        out_specs=[tile, tile],
        out_shape=[jax.ShapeDtypeStruct((b, s, d), F32), jax.ShapeDtypeStruct((b, s, d), BF16)],
        compiler_params=_params(("arbitrary", "arbitrary"), vmem),
        name="out_proj",
    )(x, mix, mod, g2, w_out)


def _mlp_kernel(x_ref, mix_ref, mod_ref, g2_ref, wo_ref, w1_ref, w2_ref, out_ref, h_ref, acc_ref):
    j = pl.program_id(2)
    tm = x_ref.shape[0]

    @pl.when(j == 0)
    def _():
        gate1 = mod_ref[2:3, :]
        shift2 = mod_ref[3:4, :]
        scale2p = 1.0 + mod_ref[4:5, :]
        g2 = g2_ref[...]
        acc_ref[...] = jnp.dot(mix_ref[...], wo_ref[...], preferred_element_type=F32)

        def body(c, carry):
            r0 = pl.multiple_of(c * ROWS, ROWS)
            xn = x_ref[pl.ds(r0, ROWS), :] + gate1 * acc_ref[pl.ds(r0, ROWS), :]
            out_ref[pl.ds(r0, ROWS), :] = xn
            h_ref[pl.ds(r0, ROWS), :] = _norm_mod(xn, g2, shift2, scale2p).astype(BF16)
            acc_ref[pl.ds(r0, ROWS), :] = jnp.zeros((ROWS, acc_ref.shape[1]), F32)
            return carry

        lax.fori_loop(0, tm // ROWS, body, 0)

    a = jnp.dot(h_ref[...], w1_ref[...], preferred_element_type=F32)
    r = jnp.square(jnp.maximum(a, 0.0)).astype(BF16)
    acc_ref[...] += jnp.dot(r, w2_ref[...], preferred_element_type=F32)

    @pl.when(j == pl.num_programs(2) - 1)
    def _():
        out_ref[...] = out_ref[...] + mod_ref[5:6, :] * acc_ref[...]


def _mlp(x, mix, mod, g2, w_out, w1, w2):
    b, s, d = x.shape
    dff = w1.shape[1]
    vmem = (2 * TM * d * 4 + 2 * TM * d * 2 + d * d * 2 + 4 * d * TF * 2 + 2 * TM * d * 4 + TM * d * 2
            + TM * d * 4 + TM * TF * 6 + (3 << 20))
    tile = pl.BlockSpec((None, TM, d), lambda i, m, j: (i, m, 0))
    return pl.pallas_call(
        _mlp_kernel,
        grid=(b, s // TM, dff // TF),
        in_specs=[
            tile,
            tile,
            pl.BlockSpec((None, 6, d), lambda i, m, j: (i, 0, 0)),
            _resident((1, d)),
            _resident((d, d)),
            pl.BlockSpec((d, TF), lambda i, m, j: (0, j)),
            pl.BlockSpec((TF, d), lambda i, m, j: (j, 0)),
        ],
        out_specs=tile,
        out_shape=jax.ShapeDtypeStruct((b, s, d), F32),
        scratch_shapes=[pltpu.VMEM((TM, d), BF16), pltpu.VMEM((TM, d), F32)],
        compiler_params=_params(("arbitrary", "arbitrary", "arbitrary"), vmem),
        name="relu2_mlp",
    )(x, mix, mod, g2, w_out, w1, w2)


def _rope_tables(s):
    pos = jnp.arange(s, dtype=F32)
    inv = ROPE_THETA ** (-jnp.arange(0, ROPE, 2, dtype=F32) / ROPE)
    ang = pos[:, None] * inv[None, :]
    cos, sin = jnp.cos(ang), jnp.sin(ang)
    return jnp.tile(cos, (1, 4)), jnp.tile(sin, (1, 4))


def _swap_halves(w):
    half = ROPE // 2
    return jnp.concatenate([-w[..., half:], w[..., :half]], axis=-1)


def _swap_gain(g):
    half = ROPE // 2
    return jnp.concatenate([g[half:], g[:half]])


def _rows8(rows):
    rows = [jnp.asarray(r, F32) for r in rows]
    rows += [jnp.zeros((LANES,), F32)] * (8 - len(rows))
    return jnp.stack(rows)


def _prep_mla(w_in, w_uq, w_ukv, qhg, khg):
    d = w_in.shape[0]
    zeros = jnp.zeros((d, ROPE), F32)
    w_kr = w_in[:, Q_RANK + KV_RANK:]
    w_krs = _swap_halves(w_kr)
    w_in2 = jnp.concatenate([
        w_in[:, :Q_RANK + KV_RANK],
        w_kr, zeros, w_krs, zeros,
        zeros, w_kr, zeros, w_krs,
    ], axis=1).astype(BF16)
    uq = w_uq.reshape(Q_RANK, N_HEADS, QK_DIM)
    uq_rope = uq[:, :, NOPE:]
    w_uq2 = jnp.concatenate([
        uq[:, :, :NOPE].reshape(Q_RANK, N_HEADS * NOPE),
        uq_rope.reshape(Q_RANK, N_HEADS * ROPE),
        _swap_halves(uq_rope).reshape(Q_RANK, N_HEADS * ROPE),
    ], axis=1).astype(BF16)
    ukv = w_ukv.reshape(KV_RANK, N_HEADS, NOPE + V_DIM)
    w_uk = ukv[:, :, :NOPE].reshape(KV_RANK, N_HEADS * NOPE).astype(BF16)
    w_uv = ukv[:, :, NOPE:].reshape(KV_RANK, N_HEADS * V_DIM).T.astype(BF16)
    z = jnp.zeros((ROPE,), F32)
    qr, kr = qhg[NOPE:], khg[NOPE:]
    qhg2 = _rows8([qhg[:NOPE], jnp.tile(qr, 2), jnp.tile(_swap_gain(qr), 2)])
    khg2 = _rows8([khg[:NOPE],
                   jnp.concatenate([kr, z]), jnp.concatenate([_swap_gain(kr), z]),
                   jnp.concatenate([z, kr]), jnp.concatenate([z, _swap_gain(kr)])])
    return w_in2, w_uq2, w_uk, w_uv, qhg2, khg2


def kernel(x, c, norm1_g, norm2_g, ada_w, ada_b, mlp_w1, mlp_w2, ab_w_in, sgu_norm_g, sgu_w, sgu_b, conv_w, conv_b,
           conv_ln_g, conv_ln_b, ab_w_out, mla_w_in, mla_q_norm_g, mla_kv_norm_g, mla_w_uq, mla_w_ukv,
           mla_q_head_g, mla_k_head_g, mla_w_out):
    b, s, d = x.shape
    depth = ada_w.shape[0]
    c_pad = jnp.pad(c.astype(F32), ((0, 8 - b), (0, 0)))
    mod = _ada(c_pad, ada_w, ada_b)[:, :b].reshape(depth, b, 6, d)
    cos2, sin2 = _rope_tables(s)

    for l in range(depth):
        g1 = norm1_g[l].reshape(1, d)
        g2 = norm2_g[l].reshape(1, d)
        if l % 2 == 0:
            e = l // 2
            mix = _even_mixer_pre(
                x, mod[l], g1, _group_in_proj(ab_w_in[e]).astype(BF16), sgu_norm_g[e].reshape(1, D_A), sgu_w[e],
                sgu_b[e].T, conv_w[e], conv_b[e].reshape(1, D_B), conv_ln_g[e].reshape(1, D_B),
                conv_ln_b[e].reshape(1, D_B))
            w_out = ab_w_out[e]
        else:
            o = l // 2
            w_in2, w_uq2, w_uk, w_uv, qhg2, khg2 = _prep_mla(
                mla_w_in[o], mla_w_uq[o], mla_w_ukv[o], mla_q_head_g[o], mla_k_head_g[o])
            qn, qr, kn, kr2, vt = _mla_pre(
                x, mod[l], g1, w_in2, mla_q_norm_g[o].reshape(1, Q_RANK), mla_kv_norm_g[o].reshape(1, KV_RANK),
                w_uq2, w_uk, w_uv, qhg2, khg2, cos2, sin2)
            mix = _attention(qn, qr, kn, kr2, vt)
            w_out = mla_w_out[o]
        x = _mlp(x, mix, mod[l], g2, w_out.astype(BF16), mlp_w1[l].astype(BF16), mlp_w2[l].astype(BF16))
    return x
```
